```python
import math
import jax
import jax.numpy as jnp
from jax import lax
import numpy as np

D_MODEL = 1024
BATCH = 2
SEQ = 16384
DEPTH = 1

MEM_LEN = 256
POOL_WIDTH = D_MODEL // 2
POOL_WINDOWS = (2, 4, 8, 16)
POOL_GROUPS = len(POOL_WINDOWS)
POOL_GROUP_WIDTH = POOL_WIDTH // POOL_GROUPS
DIFF_HEADS = 4
DIFF_HEAD_DIM = 64
DIFF_V_DIM = 2 * DIFF_HEAD_DIM
Q_WIDTH = DIFF_HEADS * 2 * DIFF_HEAD_DIM
ATTN_WIDTH = DIFF_HEADS * DIFF_V_DIM
MIX_WIDTH = POOL_WIDTH + ATTN_WIDTH
IN_PROJ_WIDTH = POOL_WIDTH + 2 * Q_WIDTH + ATTN_WIDTH + MIX_WIDTH
Q_BLOCK = 128
ROPE_THETA = 10000.0
CROSS_HEADS = 4
CROSS_HEAD_DIM = D_MODEL // CROSS_HEADS
D_FF = 4 * D_MODEL
NORM_EPS = 1e-6
NEG_BIG = -1e30

kernel_name = "hybrid_pool_diffattn_gated_block"


def rms_norm(x, g):
    xf = x.astype(jnp.float32)
    y = xf * lax.rsqrt(jnp.mean(xf * xf, axis=-1, keepdims=True) + NORM_EPS)
    return (y * g.astype(jnp.float32)).astype(x.dtype)


def lambda_init_fn(layer_idx):
    return 0.8 - 0.6 * math.exp(-0.3 * layer_idx)


def apply_rope(t):
    S = t.shape[1]
    dh = t.shape[-1]
    inv_freq = ROPE_THETA ** (-jnp.arange(0, dh, 2, dtype=jnp.float32) / dh)
    ang = jnp.arange(S, dtype=jnp.float32)[:, None] * inv_freq[None, :]
    cos = jnp.cos(ang)[None, :, None, None, :].astype(t.dtype)
    sin = jnp.sin(ang)[None, :, None, None, :].astype(t.dtype)
    t1, t2 = jnp.split(t, 2, axis=-1)
    return jnp.concatenate([t1 * cos - t2 * sin, t2 * cos + t1 * sin], axis=-1)


def causal_multiscale_pool(u, w_pool, pool_scale):
    B, S, _ = u.shape
    ug = u.reshape(B, S, POOL_GROUPS, POOL_GROUP_WIDTH)
    pos = jnp.arange(S)
    outs = []
    for g, w in enumerate(POOL_WINDOWS):
        xg = ug[:, :, g].astype(jnp.float32)
        c = jnp.cumsum(xg, axis=1)
        lag = jnp.pad(c, ((0, 0), (w, 0), (0, 0)))[:, :S]
        cnt = jnp.minimum(pos + 1, w).astype(jnp.float32)[None, :, None]
        outs.append((c - lag) / cnt - xg)
    z = jnp.stack(outs, axis=2).astype(u.dtype)
    y = jnp.einsum('bsgc,gcd->bsgd', z, w_pool).reshape(B, S, POOL_WIDTH)
    return y * pool_scale


def diff_attention(q, k, v, lam):
    B, S, H, _, Dh = q.shape
    E = v.shape[-1]
    nb = S // Q_BLOCK
    scale = Dh ** -0.5
    qb = q.reshape(B, nb, Q_BLOCK, H, 2, Dh).transpose(1, 0, 2, 3, 4, 5)
    kpos = jnp.arange(S)

    def one_block(args):
        q_blk, i = args
        qpos = i * Q_BLOCK + jnp.arange(Q_BLOCK)
        s = jnp.einsum('bqhcd,bkhcd->bhcqk', q_blk, k,
                       preferred_element_type=jnp.float32) * scale
        mask = kpos[None, :] <= qpos[:, None]
        p = jax.nn.softmax(jnp.where(mask, s, NEG_BIG), axis=-1)
        a = p[:, :, 0] - lam * p[:, :, 1]
        return jnp.einsum('bhqk,bkhe->bqhe', a.astype(v.dtype), v)

    o = lax.map(one_block, (qb, jnp.arange(nb)))
    return o.transpose(1, 0, 2, 3, 4).reshape(B, S, H, E)


def memory_cross_attention(h, mem_n, w_cq, w_ckv, w_co):
    B, S, _ = h.shape
    M = mem_n.shape[1]
    q = (h @ w_cq).reshape(B, S, CROSS_HEADS, CROSS_HEAD_DIM)
    kv = (mem_n @ w_ckv).reshape(B, M, 2, CROSS_HEADS, CROSS_HEAD_DIM)
    k, v = kv[:, :, 0], kv[:, :, 1]
    s = jnp.einsum('bshd,bmhd->bhsm', q, k,
                   preferred_element_type=jnp.float32) * CROSS_HEAD_DIM ** -0.5
    p = jax.nn.softmax(s, axis=-1).astype(v.dtype)
    o = jnp.einsum('bhsm,bmhd->bshd', p, v).reshape(B, S, D_MODEL)
    return o @ w_co


def setup_inputs(seed: int = 0) -> dict:
    key = jax.random.key(seed)
    ks = jax.random.split(key, 24)
    f32 = jnp.float32

    def nrm(k, shape, scale):
        return jax.random.normal(k, shape, f32) * scale

    def gain(k, shape):
        return 1.0 + 0.02 * jax.random.normal(k, shape, f32)

    L = DEPTH
    return {
        "x": jax.random.normal(ks[0], (BATCH, SEQ, D_MODEL), f32),
        "mem": jax.random.normal(ks[1], (BATCH, MEM_LEN, D_MODEL), f32),
        "g_mix": gain(ks[2], (L, D_MODEL)),
        "w_in": nrm(ks[3], (L, D_MODEL, IN_PROJ_WIDTH), D_MODEL ** -0.5),
        "w_pool": nrm(ks[4], (L, POOL_GROUPS, POOL_GROUP_WIDTH, POOL_GROUP_WIDTH), POOL_GROUP_WIDTH ** -0.5),
        "pool_scale": gain(ks[5], (L, POOL_WIDTH)),
        "lambda_q1": nrm(ks[6], (L, DIFF_HEAD_DIM), 0.1),
        "lambda_k1": nrm(ks[7], (L, DIFF_HEAD_DIM), 0.1),
        "lambda_q2": nrm(ks[8], (L, DIFF_HEAD_DIM), 0.1),
        "lambda_k2": nrm(ks[9], (L, DIFF_HEAD_DIM), 0.1),
        "g_subln": gain(ks[10], (L, DIFF_V_DIM)),
        "w_out": nrm(ks[11], (L, MIX_WIDTH, D_MODEL), MIX_WIDTH ** -0.5),
        "g_cross": gain(ks[12], (L, D_MODEL)),
        "g_mem": gain(ks[13], (L, D_MODEL)),
        "w_cq": nrm(ks[14], (L, D_MODEL, D_MODEL), D_MODEL ** -0.5),
        "w_ckv": nrm(ks[15], (L, D_MODEL, 2 * D_MODEL), D_MODEL ** -0.5),
        "w_co": nrm(ks[16], (L, D_MODEL, D_MODEL), D_MODEL ** -0.5),
        "g_mlp": gain(ks[17], (L, D_MODEL)),
        "w_up": nrm(ks[18], (L, D_MODEL, D_FF), D_MODEL ** -0.5),
        "w_down": nrm(ks[19], (L, D_FF, D_MODEL), D_FF ** -0.5),
        "g_final": gain(ks[20], (D_MODEL,)),
    }


def reference(x, mem, g_mix, w_in, w_pool, pool_scale, lambda_q1, lambda_k1,
              lambda_q2, lambda_k2, g_subln, w_out, g_cross, g_mem, w_cq, w_ckv,
              w_co, g_mlp, w_up, w_down, g_final):
    B, S, _ = x.shape
    splits = [POOL_WIDTH, POOL_WIDTH + Q_WIDTH, POOL_WIDTH + 2 * Q_WIDTH,
              POOL_WIDTH + 2 * Q_WIDTH + ATTN_WIDTH]
    for l in range(DEPTH):
        lam_init = lambda_init_fn(l)
        h = rms_norm(x, g_mix[l])
        proj = h @ w_in[l]
        u, q, k, v, gate_logits = jnp.split(proj, splits, axis=-1)
        y_pool = causal_multiscale_pool(u, w_pool[l], pool_scale[l])
        q = apply_rope(q.reshape(B, S, DIFF_HEADS, 2, DIFF_HEAD_DIM))
        k = apply_rope(k.reshape(B, S, DIFF_HEADS, 2, DIFF_HEAD_DIM))
        v = v.reshape(B, S, DIFF_HEADS, DIFF_V_DIM)
        lam = (jnp.exp(jnp.sum(lambda_q1[l].astype(jnp.float32) * lambda_k1[l].astype(jnp.float32)))
               - jnp.exp(jnp.sum(lambda_q2[l].astype(jnp.float32) * lambda_k2[l].astype(jnp.float32)))
               + lam_init)
        o = diff_attention(q, k, v, lam)
        o = rms_norm(o, g_subln[l]) * (1.0 - lam_init)
        y_attn = o.reshape(B, S, ATTN_WIDTH)
        gates = jax.nn.sigmoid(gate_logits.astype(jnp.float32)).astype(x.dtype)
        mixed = jnp.concatenate([y_pool, y_attn], axis=-1) * gates
        x = x + mixed @ w_out[l]
        hc = rms_norm(x, g_cross[l])
        mem_n = rms_norm(mem, g_mem[l])
        x = x + memory_cross_attention(hc, mem_n, w_cq[l], w_ckv[l], w_co[l])
        hm = rms_norm(x, g_mlp[l])
        x = x + jnp.square(jax.nn.relu(hm @ w_up[l])) @ w_down[l]
    return rms_norm(x, g_final)
```

```python
import functools
import math

import jax
import jax.numpy as jnp
from jax import lax
from jax.experimental import pallas as pl
from jax.experimental.pallas import tpu as pltpu

D_MODEL = 1024
POOL_WIDTH = 512
POOL_WINDOWS = (2, 4, 8, 16)
POOL_GROUP_WIDTH = 128
POOL_HALO = 16
DIFF_HEADS = 4
DIFF_HEAD_DIM = 64
HEAD_WIDTH = 2 * DIFF_HEAD_DIM
Q_WIDTH = 512
ATTN_WIDTH = 512
ROPE_THETA = 10000.0
CROSS_HEADS = 4
CROSS_HEAD_DIM = 256
D_FF = 4096
NORM_EPS = 1e-6
NEG_BIG = -1e30
LAM_INIT = 0.8 - 0.6 * math.exp(-0.3 * 0)
LOG2E = math.log2(math.e)

LANES = 128
VMEM_LIMIT = 56 * 1024 * 1024

PROJ_TILE = 512
ATTN_TQ = 512
ATTN_TK = 512
ROW_TILE = 512
FF_CHUNK = 1024

BF16 = jnp.bfloat16
F32 = jnp.float32


def _rms(x, g):
    return x * lax.rsqrt(jnp.mean(x * x, axis=-1, keepdims=True) + NORM_EPS) * g


def _rotate_half_pairs(x):
    lane = lax.broadcasted_iota(jnp.int32, x.shape, 1)
    lower = (lane % DIFF_HEAD_DIM) < (DIFF_HEAD_DIM // 2)
    return jnp.where(lower, pltpu.roll(x, LANES - 32, axis=1), pltpu.roll(x, 32, axis=1))


def _in_proj_kernel(x_ref, g_ref, w_ref, wpool_ref, pscale_ref, cos_ref, sin_ref,
                    mp_ref, q_ref, k_ref, v_ref, ga_ref, ext_ref):
    si = pl.program_id(1)
    ts = x_ref.shape[1]
    h = _rms(x_ref[0], g_ref[...]).astype(BF16)

    def proj(lo, hi):
        return jnp.dot(h, w_ref[:, lo:hi], preferred_element_type=F32)

    gates = 1.0 / (1.0 + jnp.exp(-proj(2048, 3072)))
    ga_ref[0] = gates[:, POOL_WIDTH:].astype(BF16)

    @pl.when(si == 0)
    def _():
        ext_ref[0:POOL_HALO, :] = jnp.zeros((POOL_HALO, POOL_WIDTH), F32)

    ext_ref[POOL_HALO:, :] = proj(0, POOL_WIDTH)
    pos = si * ts + lax.broadcasted_iota(jnp.int32, (ts, 1), 0)
    for g, w in enumerate(POOL_WINDOWS):
        lo, hi = g * POOL_GROUP_WIDTH, (g + 1) * POOL_GROUP_WIDTH
        e = ext_ref[:, lo:hi]
        win = e
        shift = 1
        while shift < w:
            win = win + pltpu.roll(win, shift, axis=0)
            shift *= 2
        cnt = jnp.minimum(pos + 1, w).astype(F32)
        z = win[POOL_HALO:] / cnt - e[POOL_HALO:]
        y = jnp.dot(z.astype(BF16), wpool_ref[g], preferred_element_type=F32)
        y = y * pscale_ref[:, lo:hi] * gates[:, lo:hi]
        mp_ref[0, :, lo:hi] = y.astype(BF16)
    ext_ref[0:POOL_HALO, :] = ext_ref[ts:ts + POOL_HALO, :]

    cos = cos_ref[...]
    sin = sin_ref[...]
    qscale = DIFF_HEAD_DIM ** -0.5 * LOG2E
    q = proj(512, 1024)
    k = proj(1024, 1536)
    for c in range(Q_WIDTH // LANES):
        sl = slice(c * LANES, (c + 1) * LANES)
        qc = q[:, sl]
        kc = k[:, sl]
        q_ref[0, :, sl] = ((qc * cos + _rotate_half_pairs(qc) * sin) * qscale).astype(BF16)
        k_ref[0, :, sl] = (kc * cos + _rotate_half_pairs(kc) * sin).astype(BF16)
    v_ref[0] = proj(1536, 2048).astype(BF16)


def _in_proj(x, g_mix, w_in, w_pool, pool_scale, cos_t, sin_t):
    B, S, D = x.shape
    ts = PROJ_TILE
    row = lambda b, s: (b, s, 0)
    const2 = lambda b, s: (0, 0)
    half = pl.BlockSpec((1, ts, 512), row)
    out_sds = jax.ShapeDtypeStruct((B, S, 512), BF16)
    return pl.pallas_call(
        _in_proj_kernel,
        grid=(B, S // ts),
        in_specs=[
            pl.BlockSpec((1, ts, D), row),
            pl.BlockSpec((1, D), const2),
            pl.BlockSpec((D, 3072), const2),
            pl.BlockSpec((4, 128, 128), lambda b, s: (0, 0, 0)),
            pl.BlockSpec((1, POOL_WIDTH), const2),
            pl.BlockSpec((ts, LANES), lambda b, s: (s, 0)),
            pl.BlockSpec((ts, LANES), lambda b, s: (s, 0)),
        ],
        out_specs=[half, half, half, half, half],
        out_shape=[out_sds] * 5,
        scratch_shapes=[pltpu.VMEM((ts + POOL_HALO, POOL_WIDTH), F32)],
        compiler_params=pltpu.CompilerParams(
            dimension_semantics=("arbitrary", "arbitrary"),
            vmem_limit_bytes=VMEM_LIMIT),
        name="in_proj",
    )(x, g_mix, w_in, w_pool, pool_scale, cos_t, sin_t)


def _mem_kv_kernel(mem_ref, g_ref, w_ref, kv_ref):
    mn = _rms(mem_ref[0], g_ref[...]).astype(BF16)
    kv_ref[0] = jnp.dot(mn, w_ref[...], preferred_element_type=F32).astype(BF16)


def _mem_kv(mem, g_mem, w_ckv):
    B, M, D = mem.shape
    return pl.pallas_call(
        _mem_kv_kernel,
        grid=(B,),
        in_specs=[
            pl.BlockSpec((1, M, D), lambda b: (b, 0, 0)),
            pl.BlockSpec((1, D), lambda b: (0, 0)),
            pl.BlockSpec((D, 2 * D), lambda b: (0, 0)),
        ],
        out_specs=pl.BlockSpec((1, M, 2 * D), lambda b: (b, 0, 0)),
        out_shape=jax.ShapeDtypeStruct((B, M, 2 * D), BF16),
        compiler_params=pltpu.CompilerParams(
            dimension_semantics=("arbitrary",), vmem_limit_bytes=VMEM_LIMIT),
        name="mem_kv",
    )(mem, g_mem, w_ckv)


def _diff_attn_kernel(q_ref, k_ref, v_ref, ga_ref, gsub_ref, lq1_ref, lk1_ref,
                      lq2_ref, lk2_ref, o_ref, qs_ref, m_ref, l_ref, acc_ref):
    qi = pl.program_id(2)
    tq = q_ref.shape[1]
    tk = ATTN_TK

    q = q_ref[0]
    lane = lax.broadcasted_iota(jnp.int32, q.shape, 1)
    first = lane < DIFF_HEAD_DIM
    qs_ref[0:tq, :] = jnp.where(first, q, jnp.zeros_like(q))
    qs_ref[tq:, :] = jnp.where(first, jnp.zeros_like(q), q)
    m_ref[...] = jnp.full(m_ref.shape, NEG_BIG, F32)
    l_ref[...] = jnp.zeros(l_ref.shape, F32)
    acc_ref[...] = jnp.zeros(acc_ref.shape, F32)

    def block(kb, masked):
        start = pl.multiple_of(kb * tk, tk)
        kblk = k_ref[0, pl.ds(start, tk), :]
        vblk = v_ref[0, pl.ds(start, tk), :]
        s = lax.dot_general(qs_ref[...], kblk, (((1,), (1,)), ((), ())),
                            preferred_element_type=F32)
        if masked:
            r = lax.broadcasted_iota(jnp.int32, s.shape, 0) % tq
            c = lax.broadcasted_iota(jnp.int32, s.shape, 1)
            s = jnp.where(c <= r, s, NEG_BIG)
        m_prev = m_ref[...]
        m_new = jnp.maximum(m_prev, jnp.max(s, axis=1, keepdims=True))
        alpha = jnp.exp2(m_prev - m_new)
        p = jnp.exp2(s - pltpu.repeat(m_new, tk // LANES, axis=1))
        l_ref[...] = alpha * l_ref[...] + jnp.sum(p, axis=1, keepdims=True)
        acc_ref[...] = alpha * acc_ref[...] + jnp.dot(
            p.astype(BF16), vblk, preferred_element_type=F32)
        m_ref[...] = m_new

    def body(kb, carry):
        block(kb, False)
        return carry

    lax.fori_loop(0, qi, body, 0)
    block(qi, True)

    lam = (jnp.exp(jnp.sum(lq1_ref[...] * lk1_ref[...], keepdims=True))
           - jnp.exp(jnp.sum(lq2_ref[...] * lk2_ref[...], keepdims=True)) + LAM_INIT)
    o1 = acc_ref[0:tq, :] / l_ref[0:tq, :]
    o2 = acc_ref[tq:, :] / l_ref[tq:, :]
    o = o1 - lam * o2
    y = _rms(o, gsub_ref[...]) * (1.0 - LAM_INIT)
    o_ref[0] = (y * ga_ref[0].astype(F32)).astype(BF16)


def _diff_attn(q, k, v, ga, g_subln, lq1, lk1, lq2, lk2):
    B, S, _ = q.shape
    tq = ATTN_TQ
    assert tq == ATTN_TK
    qblk = pl.BlockSpec((1, tq, HEAD_WIDTH), lambda b, h, i: (b, i, h))
    kvblk = pl.BlockSpec((1, S, HEAD_WIDTH), lambda b, h, i: (b, 0, h))
    vec = lambda n: pl.BlockSpec((1, n), lambda b, h, i: (0, 0))
    return pl.pallas_call(
        _diff_attn_kernel,
        grid=(B, DIFF_HEADS, S // tq),
        in_specs=[qblk, kvblk, kvblk, qblk, vec(HEAD_WIDTH),
                  vec(DIFF_HEAD_DIM), vec(DIFF_HEAD_DIM), vec(DIFF_HEAD_DIM), vec(DIFF_HEAD_DIM)],
        out_specs=qblk,
        out_shape=jax.ShapeDtypeStruct((B, S, ATTN_WIDTH), BF16),
        scratch_shapes=[
            pltpu.VMEM((2 * tq, HEAD_WIDTH), BF16),
            pltpu.VMEM((2 * tq, LANES), F32),
            pltpu.VMEM((2 * tq, LANES), F32),
            pltpu.VMEM((2 * tq, HEAD_WIDTH), F32),
        ],
        compiler_params=pltpu.CompilerParams(
            dimension_semantics=("arbitrary", "arbitrary", "arbitrary"),
            vmem_limit_bytes=VMEM_LIMIT),
        name="diff_attn",
    )(q, k, v, ga, g_subln, lq1, lk1, lq2, lk2)


def _mix_cross_kernel(x_ref, mp_ref, ya_ref, wout_ref, gc_ref, wcq_ref, kv_ref, wco_ref,
                      o_ref, oc_ref):
    x1 = (x_ref[0]
          + jnp.dot(mp_ref[0], wout_ref[0:POOL_WIDTH, :], preferred_element_type=F32)
          + jnp.dot(ya_ref[0], wout_ref[POOL_WIDTH:, :], preferred_element_type=F32))
    hc = _rms(x1, gc_ref[...]).astype(BF16)
    qc = (jnp.dot(hc, wcq_ref[...], preferred_element_type=F32)
          * CROSS_HEAD_DIM ** -0.5).astype(BF16)
    for h in range(CROSS_HEADS):
        sl = slice(h * CROSS_HEAD_DIM, (h + 1) * CROSS_HEAD_DIM)
        kh = kv_ref[0, :, sl]
        vh = kv_ref[0, :, D_MODEL + h * CROSS_HEAD_DIM:D_MODEL + (h + 1) * CROSS_HEAD_DIM]
        s = lax.dot_general(qc[:, sl], kh, (((1,), (1,)), ((), ())),
                            preferred_element_type=F32)
        p = jnp.exp(s - jnp.max(s, axis=1, keepdims=True))
        l = jnp.sum(p, axis=1, keepdims=True)
        oh = jnp.dot(p.astype(BF16), vh, preferred_element_type=F32) / l
        oc_ref[:, sl] = oh.astype(BF16)
    o_ref[0] = x1 + jnp.dot(oc_ref[...], wco_ref[...], preferred_element_type=F32)


def _mix_cross(x, mp, ya, w_out, g_cross, w_cq, kv, w_co):
    B, S, D = x.shape
    ts = ROW_TILE
    M = kv.shape[1]
    row = lambda b, s: (b, s, 0)
    const2 = lambda b, s: (0, 0)
    wspec = pl.BlockSpec((D, D), const2)
    return pl.pallas_call(
        _mix_cross_kernel,
        grid=(B, S // ts),
        in_specs=[
            pl.BlockSpec((1, ts, D), row),
            pl.BlockSpec((1, ts, POOL_WIDTH), row),
            pl.BlockSpec((1, ts, ATTN_WIDTH), row),
            wspec,
            pl.BlockSpec((1, D), const2),
            wspec,
            pl.BlockSpec((1, M, 2 * D), lambda b, s: (b, 0, 0)),
            wspec,
        ],
        out_specs=pl.BlockSpec((1, ts, D), row),
        out_shape=jax.ShapeDtypeStruct((B, S, D), F32),
        scratch_shapes=[pltpu.VMEM((ts, D), BF16)],
        compiler_params=pltpu.CompilerParams(
            dimension_semantics=("arbitrary", "arbitrary"),
            vmem_limit_bytes=VMEM_LIMIT),
        name="mix_cross",
    )(x, mp, ya, w_out, g_cross, w_cq, kv, w_co)


def _mlp_kernel(x_ref, gm_ref, wup_ref, wdn_ref, gf_ref, o_ref):
    x2 = x_ref[0]
    hm = _rms(x2, gm_ref[...]).astype(BF16)
    acc = x2
    for c in range(D_FF // FF_CHUNK):
        sl = slice(c * FF_CHUNK, (c + 1) * FF_CHUNK)
        up = jnp.dot(hm, wup_ref[:, sl], preferred_element_type=F32)
        a = jnp.square(jnp.maximum(up, 0.0)).astype(BF16)
        acc = acc + jnp.dot(a, wdn_ref[sl, :], preferred_element_type=F32)
    o_ref[0] = _rms(acc, gf_ref[...])


def _mlp(x2, g_mlp, w_up, w_down, g_final):
    B, S, D = x2.shape
    ts = ROW_TILE
    row = lambda b, s: (b, s, 0)
    const2 = lambda b, s: (0, 0)
    return pl.pallas_call(
        _mlp_kernel,
        grid=(B, S // ts),
        in_specs=[
            pl.BlockSpec((1, ts, D), row),
            pl.BlockSpec((1, D), const2),
            pl.BlockSpec((D, D_FF), const2),
            pl.BlockSpec((D_FF, D), const2),
            pl.BlockSpec((1, D), const2),
        ],
        out_specs=pl.BlockSpec((1, ts, D), row),
        out_shape=jax.ShapeDtypeStruct((B, S, D), F32),
        compiler_params=pltpu.CompilerParams(
            dimension_semantics=("arbitrary", "arbitrary"),
            vmem_limit_bytes=VMEM_LIMIT),
        name="mlp",
    )(x2, g_mlp, w_up, w_down, g_final)


def _rope_tables(S):
    dh = DIFF_HEAD_DIM
    inv_freq = ROPE_THETA ** (-jnp.arange(0, dh, 2, dtype=F32) / dh)
    ang = jnp.arange(S, dtype=F32)[:, None] * inv_freq[None, :]
    cos = jnp.cos(ang)
    sin = jnp.sin(ang)
    cos_t = jnp.concatenate([cos, cos, cos, cos], axis=-1)
    sin_t = jnp.concatenate([-sin, sin, -sin, sin], axis=-1)
    return cos_t, sin_t


def kernel(x, mem, g_mix, w_in, w_pool, pool_scale, lambda_q1, lambda_k1, lambda_q2,
           lambda_k2, g_subln, w_out, g_cross, g_mem, w_cq, w_ckv, w_co, g_mlp, w_up,
           w_down, g_final):
    B, S, _ = x.shape
    assert g_mix.shape[0] == 1, "single-layer block"
    cos_t, sin_t = _rope_tables(S)
    mp, q, k, v, ga = _in_proj(x, g_mix, w_in[0].astype(BF16), w_pool[0].astype(BF16),
                               pool_scale, cos_t, sin_t)
    kv = _mem_kv(mem, g_mem, w_ckv[0].astype(BF16))
    ya = _diff_attn(q, k, v, ga, g_subln, lambda_q1, lambda_k1, lambda_q2, lambda_k2)
    x2 = _mix_cross(x, mp, ya, w_out[0].astype(BF16), g_cross, w_cq[0].astype(BF16), kv,
                    w_co[0].astype(BF16))
    return _mlp(x2, g_mlp, w_up[0].astype(BF16), w_down[0].astype(BF16),
                g_final.reshape(1, D_MODEL))
```

```python
import math

import jax
import jax.numpy as jnp
from jax import lax
from jax.experimental import pallas as pl
from jax.experimental.pallas import tpu as pltpu

D_MODEL = 1024
POOL_WIDTH = 512
POOL_WINDOWS = (2, 4, 8, 16)
POOL_GROUP_WIDTH = 128
POOL_HALO = 16
DIFF_HEADS = 4
DIFF_HEAD_DIM = 64
HEAD_WIDTH = 2 * DIFF_HEAD_DIM
Q_WIDTH = 512
ATTN_WIDTH = 512
ROPE_THETA = 10000.0
CROSS_HEADS = 4
CROSS_HEAD_DIM = 256
D_FF = 4096
NORM_EPS = 1e-6
NEG_BIG = -1e30
LAM_INIT = 0.8 - 0.6 * math.exp(-0.3 * 0)
LOG2E = math.log2(math.e)

LANES = 128
BF16_SUBLANES = 16
VMEM_LIMIT = 56 * 1024 * 1024

PROJ_TILE = 512
ATTN_TQ = 512
ATTN_TK = 512
ROW_TILE = 512
FF_CHUNK = 1024
VT_ROWS = HEAD_WIDTH + BF16_SUBLANES

BF16 = jnp.bfloat16
F32 = jnp.float32


def _rms(x, g):
    return x * lax.rsqrt(jnp.mean(x * x, axis=-1, keepdims=True) + NORM_EPS) * g


def _rotate_half_pairs(x):
    lane = lax.broadcasted_iota(jnp.int32, x.shape, 1)
    lower = (lane % DIFF_HEAD_DIM) < (DIFF_HEAD_DIM // 2)
    return jnp.where(lower, pltpu.roll(x, LANES - 32, axis=1), pltpu.roll(x, 32, axis=1))


def _in_proj_kernel(x_ref, g_ref, w_ref, wpool_ref, pscale_ref, cos_ref, sin_ref,
                    mp_ref, q_ref, k_ref, vt_ref, ga_ref, ext_ref):
    si = pl.program_id(1)
    ts = x_ref.shape[1]
    h = _rms(x_ref[0], g_ref[...]).astype(BF16)

    def proj(lo, hi):
        return jnp.dot(h, w_ref[:, lo:hi], preferred_element_type=F32)

    gates = 1.0 / (1.0 + jnp.exp(-proj(2048, 3072)))
    ga_ref[0] = gates[:, POOL_WIDTH:].astype(BF16)

    @pl.when(si == 0)
    def _():
        ext_ref[0:POOL_HALO, :] = jnp.zeros((POOL_HALO, POOL_WIDTH), F32)

    ext_ref[POOL_HALO:, :] = proj(0, POOL_WIDTH)
    pos = si * ts + lax.broadcasted_iota(jnp.int32, (ts, 1), 0)
    for g, w in enumerate(POOL_WINDOWS):
        lo, hi = g * POOL_GROUP_WIDTH, (g + 1) * POOL_GROUP_WIDTH
        e = ext_ref[:, lo:hi]
        win = e
        shift = 1
        while shift < w:
            win = win + pltpu.roll(win, shift, axis=0)
            shift *= 2
        cnt = jnp.minimum(pos + 1, w).astype(F32)
        z = win[POOL_HALO:] / cnt - e[POOL_HALO:]
        y = jnp.dot(z.astype(BF16), wpool_ref[g], preferred_element_type=F32)
        y = y * pscale_ref[:, lo:hi] * gates[:, lo:hi]
        mp_ref[0, :, lo:hi] = y.astype(BF16)
    ext_ref[0:POOL_HALO, :] = ext_ref[ts:ts + POOL_HALO, :]

    cos = cos_ref[...]
    sin = sin_ref[...]
    qscale = DIFF_HEAD_DIM ** -0.5 * LOG2E
    q = proj(512, 1024)
    k = proj(1024, 1536)
    for c in range(Q_WIDTH // LANES):
        sl = slice(c * LANES, (c + 1) * LANES)
        qc = q[:, sl]
        kc = k[:, sl]
        q_ref[0, :, sl] = ((qc * cos + _rotate_half_pairs(qc) * sin) * qscale).astype(BF16)
        k_ref[0, :, sl] = (kc * cos + _rotate_half_pairs(kc) * sin).astype(BF16)

    v = proj(1536, 2048)
    for hd in range(DIFF_HEADS):
        vt_ref[0, hd] = v[:, hd * HEAD_WIDTH:(hd + 1) * HEAD_WIDTH].T.astype(BF16)


def _in_proj(x, g_mix, w_in, w_pool, pool_scale, cos_t, sin_t):
    B, S, D = x.shape
    ts = PROJ_TILE
    row = lambda b, s: (b, s, 0)
    const2 = lambda b, s: (0, 0)
    half = pl.BlockSpec((1, ts, 512), row)
    half_sds = jax.ShapeDtypeStruct((B, S, 512), BF16)
    vt_spec = pl.BlockSpec((1, DIFF_HEADS, HEAD_WIDTH, ts), lambda b, s: (b, 0, 0, s))
    vt_sds = jax.ShapeDtypeStruct((B, DIFF_HEADS, HEAD_WIDTH, S), BF16)
    return pl.pallas_call(
        _in_proj_kernel,
        grid=(B, S // ts),
        in_specs=[
            pl.BlockSpec((1, ts, D), row),
            pl.BlockSpec((1, D), const2),
            pl.BlockSpec((D, 3072), const2),
            pl.BlockSpec((4, 128, 128), lambda b, s: (0, 0, 0)),
            pl.BlockSpec((1, POOL_WIDTH), const2),
            pl.BlockSpec((ts, LANES), lambda b, s: (s, 0)),
            pl.BlockSpec((ts, LANES), lambda b, s: (s, 0)),
        ],
        out_specs=[half, half, half, vt_spec, half],
        out_shape=[half_sds, half_sds, half_sds, vt_sds, half_sds],
        scratch_shapes=[pltpu.VMEM((ts + POOL_HALO, POOL_WIDTH), F32)],
        compiler_params=pltpu.CompilerParams(
            dimension_semantics=("arbitrary", "arbitrary"),
            vmem_limit_bytes=VMEM_LIMIT),
        name="in_proj",
    )(x, g_mix, w_in, w_pool, pool_scale, cos_t, sin_t)


def _mem_kv_kernel(mem_ref, g_ref, w_ref, kv_ref):
    mn = _rms(mem_ref[0], g_ref[...]).astype(BF16)
    kv_ref[0] = jnp.dot(mn, w_ref[...], preferred_element_type=F32).astype(BF16)


def _mem_kv(mem, g_mem, w_ckv):
    B, M, D = mem.shape
    return pl.pallas_call(
        _mem_kv_kernel,
        grid=(B,),
        in_specs=[
            pl.BlockSpec((1, M, D), lambda b: (b, 0, 0)),
            pl.BlockSpec((1, D), lambda b: (0, 0)),
            pl.BlockSpec((D, 2 * D), lambda b: (0, 0)),
        ],
        out_specs=pl.BlockSpec((1, M, 2 * D), lambda b: (b, 0, 0)),
        out_shape=jax.ShapeDtypeStruct((B, M, 2 * D), BF16),
        compiler_params=pltpu.CompilerParams(
            dimension_semantics=("arbitrary",), vmem_limit_bytes=VMEM_LIMIT),
        name="mem_kv",
    )(mem, g_mem, w_ckv)


def _diff_attn_kernel(q_ref, k_ref, vt_ref, ga_ref, gsub_ref, lq1_ref, lk1_ref,
                      lq2_ref, lk2_ref, o_ref, qs_ref, vx_ref, m_ref, acc_ref, st0_ref):
    qi = pl.program_id(2)
    tq = q_ref.shape[1]
    tk = ATTN_TK
    nkb = vx_ref.shape[0]

    @pl.when(qi == 0)
    def _():
        for j in range(nkb):
            vx_ref[j, 0:HEAD_WIDTH, :] = vt_ref[0, 0, :, j * tk:(j + 1) * tk]
            vx_ref[j, HEAD_WIDTH:, :] = jnp.ones((BF16_SUBLANES, tk), BF16)

    q = q_ref[0]
    lane = lax.broadcasted_iota(jnp.int32, q.shape, 1)
    first = lane < DIFF_HEAD_DIM
    qs_ref[0:tq, :] = jnp.where(first, q, jnp.zeros_like(q))
    qs_ref[tq:, :] = jnp.where(first, jnp.zeros_like(q), q)
    m_ref[...] = jnp.full(m_ref.shape, NEG_BIG, F32)
    acc_ref[...] = jnp.zeros(acc_ref.shape, F32)

    def scores(kb, c):
        start = pl.multiple_of(kb * tk, tk)
        return lax.dot_general(k_ref[0, pl.ds(start, tk), :], qs_ref[c * tq:(c + 1) * tq, :],
                               (((1,), (1,)), ((), ())),
                               preferred_element_type=F32)

    def softmax(st, c, masked):
        cols = slice(c * tq, (c + 1) * tq)
        if masked:
            key = lax.broadcasted_iota(jnp.int32, st.shape, 0)
            qry = lax.broadcasted_iota(jnp.int32, st.shape, 1)
            st = jnp.where(key <= qry, st, NEG_BIG)
        m_prev = m_ref[:, cols]
        m_new = jnp.maximum(m_prev, jnp.max(st, axis=0, keepdims=True))
        m_ref[:, cols] = m_new
        return jnp.exp2(m_prev - m_new), jnp.exp2(st - m_new).astype(BF16)

    def pv(kb, c, alpha, pt):
        cols = slice(c * tq, (c + 1) * tq)
        acc_ref[:, cols] = alpha * acc_ref[:, cols] + jnp.dot(
            vx_ref[kb], pt, preferred_element_type=F32)

    def block(kb, masked):
        st1 = scores(kb, 1)
        a0, p0 = softmax(st0_ref[...], 0, masked)
        pv(kb, 0, a0, p0)
        if not masked:
            st0_ref[...] = scores(kb + 1, 0)
        a1, p1 = softmax(st1, 1, masked)
        pv(kb, 1, a1, p1)

    st0_ref[...] = scores(0, 0)

    def body(kb, carry):
        block(kb, False)
        return carry

    lax.fori_loop(0, qi, body, 0)
    block(qi, True)

    lam = (jnp.exp(jnp.sum(lq1_ref[...] * lk1_ref[...], keepdims=True))
           - jnp.exp(jnp.sum(lq2_ref[...] * lk2_ref[...], keepdims=True)) + LAM_INIT)
    o1 = acc_ref[0:HEAD_WIDTH, 0:tq] / acc_ref[HEAD_WIDTH:HEAD_WIDTH + 1, 0:tq]
    o2 = acc_ref[0:HEAD_WIDTH, tq:] / acc_ref[HEAD_WIDTH:HEAD_WIDTH + 1, tq:]
    ot = o1 - lam * o2
    inv = lax.rsqrt(jnp.mean(ot * ot, axis=0, keepdims=True) + NORM_EPS)
    yt = ot * inv * gsub_ref[...] * (1.0 - LAM_INIT)
    o_ref[0] = (yt.T * ga_ref[0].astype(F32)).astype(BF16)


def _diff_attn(q, k, vt, ga, g_subln_col, lq1, lk1, lq2, lk2):
    B, S, _ = q.shape
    tq = ATTN_TQ
    assert tq == ATTN_TK
    qblk = pl.BlockSpec((1, tq, HEAD_WIDTH), lambda b, h, i: (b, i, h))
    kblk = pl.BlockSpec((1, S, HEAD_WIDTH), lambda b, h, i: (b, 0, h))
    vtblk = pl.BlockSpec((1, 1, HEAD_WIDTH, S), lambda b, h, i: (b, h, 0, 0))
    vec = lambda n: pl.BlockSpec((1, n), lambda b, h, i: (0, 0))
    return pl.pallas_call(
        _diff_attn_kernel,
        grid=(B, DIFF_HEADS, S // tq),
        in_specs=[qblk, kblk, vtblk, qblk,
                  pl.BlockSpec((HEAD_WIDTH, 1), lambda b, h, i: (0, 0)),
                  vec(DIFF_HEAD_DIM), vec(DIFF_HEAD_DIM), vec(DIFF_HEAD_DIM), vec(DIFF_HEAD_DIM)],
        out_specs=qblk,
        out_shape=jax.ShapeDtypeStruct((B, S, ATTN_WIDTH), BF16),
        scratch_shapes=[
            pltpu.VMEM((2 * tq, HEAD_WIDTH), BF16),
            pltpu.VMEM((S // ATTN_TK, VT_ROWS, ATTN_TK), BF16),
            pltpu.VMEM((1, 2 * tq), F32),
            pltpu.VMEM((VT_ROWS, 2 * tq), F32),
            pltpu.VMEM((ATTN_TK, tq), F32),
        ],
        compiler_params=pltpu.CompilerParams(
            dimension_semantics=("arbitrary", "arbitrary", "arbitrary"),
            vmem_limit_bytes=VMEM_LIMIT),
        name="diff_attn",
    )(q, k, vt, ga, g_subln_col, lq1, lk1, lq2, lk2)


def _mix_cross_kernel(x_ref, mp_ref, ya_ref, wout_ref, gc_ref, wcq_ref, kv_ref, wco_ref,
                      o_ref, oc_ref):
    x1 = (x_ref[0]
          + jnp.dot(mp_ref[0], wout_ref[0:POOL_WIDTH, :], preferred_element_type=F32)
          + jnp.dot(ya_ref[0], wout_ref[POOL_WIDTH:, :], preferred_element_type=F32))
    hc = _rms(x1, gc_ref[...]).astype(BF16)
    qc = (jnp.dot(hc, wcq_ref[...], preferred_element_type=F32)
          * CROSS_HEAD_DIM ** -0.5).astype(BF16)
    for h in range(CROSS_HEADS):
        sl = slice(h * CROSS_HEAD_DIM, (h + 1) * CROSS_HEAD_DIM)
        kh = kv_ref[0, :, sl]
        vh = kv_ref[0, :, D_MODEL + h * CROSS_HEAD_DIM:D_MODEL + (h + 1) * CROSS_HEAD_DIM]
        s = lax.dot_general(qc[:, sl], kh, (((1,), (1,)), ((), ())),
                            preferred_element_type=F32)
        p = jnp.exp(s - jnp.max(s, axis=1, keepdims=True))
        l = jnp.sum(p, axis=1, keepdims=True)
        oh = jnp.dot(p.astype(BF16), vh, preferred_element_type=F32) / l
        oc_ref[:, sl] = oh.astype(BF16)
    o_ref[0] = x1 + jnp.dot(oc_ref[...], wco_ref[...], preferred_element_type=F32)


def _mix_cross(x, mp, ya, w_out, g_cross, w_cq, kv, w_co):
    B, S, D = x.shape
    ts = ROW_TILE
    M = kv.shape[1]
    row = lambda b, s: (b, s, 0)
    const2 = lambda b, s: (0, 0)
    wspec = pl.BlockSpec((D, D), const2)
    return pl.pallas_call(
        _mix_cross_kernel,
        grid=(B, S // ts),
        in_specs=[
            pl.BlockSpec((1, ts, D), row),
            pl.BlockSpec((1, ts, POOL_WIDTH), row),
            pl.BlockSpec((1, ts, ATTN_WIDTH), row),
            wspec,
            pl.BlockSpec((1, D), const2),
            wspec,
            pl.BlockSpec((1, M, 2 * D), lambda b, s: (b, 0, 0)),
            wspec,
        ],
        out_specs=pl.BlockSpec((1, ts, D), row),
        out_shape=jax.ShapeDtypeStruct((B, S, D), F32),
        scratch_shapes=[pltpu.VMEM((ts, D), BF16)],
        compiler_params=pltpu.CompilerParams(
            dimension_semantics=("arbitrary", "arbitrary"),
            vmem_limit_bytes=VMEM_LIMIT),
        name="mix_cross",
    )(x, mp, ya, w_out, g_cross, w_cq, kv, w_co)


def _mlp_kernel(x_ref, gm_ref, wup_ref, wdn_ref, gf_ref, o_ref):
    x2 = x_ref[0]
    hm = _rms(x2, gm_ref[...]).astype(BF16)
    acc = x2
    for c in range(D_FF // FF_CHUNK):
        sl = slice(c * FF_CHUNK, (c + 1) * FF_CHUNK)
        up = jnp.dot(hm, wup_ref[:, sl], preferred_element_type=F32)
        a = jnp.square(jnp.maximum(up, 0.0)).astype(BF16)
        acc = acc + jnp.dot(a, wdn_ref[sl, :], preferred_element_type=F32)
    o_ref[0] = _rms(acc, gf_ref[...])


def _mlp(x2, g_mlp, w_up, w_down, g_final):
    B, S, D = x2.shape
    ts = ROW_TILE
    row = lambda b, s: (b, s, 0)
    const2 = lambda b, s: (0, 0)
    return pl.pallas_call(
        _mlp_kernel,
        grid=(B, S // ts),
        in_specs=[
            pl.BlockSpec((1, ts, D), row),
            pl.BlockSpec((1, D), const2),
            pl.BlockSpec((D, D_FF), const2),
            pl.BlockSpec((D_FF, D), const2),
            pl.BlockSpec((1, D), const2),
        ],
        out_specs=pl.BlockSpec((1, ts, D), row),
        out_shape=jax.ShapeDtypeStruct((B, S, D), F32),
        compiler_params=pltpu.CompilerParams(
            dimension_semantics=("arbitrary", "arbitrary"),
            vmem_limit_bytes=VMEM_LIMIT),
        name="mlp",
    )(x2, g_mlp, w_up, w_down, g_final)


def _rope_tables(S):
    dh = DIFF_HEAD_DIM
    inv_freq = ROPE_THETA ** (-jnp.arange(0, dh, 2, dtype=F32) / dh)
    ang = jnp.arange(S, dtype=F32)[:, None] * inv_freq[None, :]
    cos = jnp.cos(ang)
    sin = jnp.sin(ang)
    cos_t = jnp.concatenate([cos, cos, cos, cos], axis=-1)
    sin_t = jnp.concatenate([-sin, sin, -sin, sin], axis=-1)
    return cos_t, sin_t


def kernel(x, mem, g_mix, w_in, w_pool, pool_scale, lambda_q1, lambda_k1, lambda_q2,
           lambda_k2, g_subln, w_out, g_cross, g_mem, w_cq, w_ckv, w_co, g_mlp, w_up,
           w_down, g_final):
    B, S, _ = x.shape
    assert g_mix.shape[0] == 1, "single-layer block"
    cos_t, sin_t = _rope_tables(S)
    mp, q, k, vt, ga = _in_proj(x, g_mix, w_in[0].astype(BF16), w_pool[0].astype(BF16),
                                pool_scale, cos_t, sin_t)
    kv = _mem_kv(mem, g_mem, w_ckv[0].astype(BF16))
    ya = _diff_attn(q, k, vt, ga, g_subln.reshape(HEAD_WIDTH, 1), lambda_q1, lambda_k1,
                    lambda_q2, lambda_k2)
    x2 = _mix_cross(x, mp, ya, w_out[0].astype(BF16), g_cross, w_cq[0].astype(BF16), kv,
                    w_co[0].astype(BF16))
    return _mlp(x2, g_mlp, w_up[0].astype(BF16), w_down[0].astype(BF16),
                g_final.reshape(1, D_MODEL))
```

```python
import math

import jax
import jax.numpy as jnp
from jax import lax
from jax.experimental import pallas as pl
from jax.experimental.pallas import tpu as pltpu

D_MODEL = 1024
POOL_WIDTH = 512
POOL_WINDOWS = (2, 4, 8, 16)
POOL_GROUP_WIDTH = 128
POOL_HALO = 16
DIFF_HEADS = 4
DIFF_HEAD_DIM = 64
HEAD_WIDTH = 2 * DIFF_HEAD_DIM
Q_WIDTH = 512
ATTN_WIDTH = 512
ROPE_THETA = 10000.0
CROSS_HEADS = 4
CROSS_HEAD_DIM = 256
D_FF = 4096
NORM_EPS = 1e-6
NEG_BIG = -1e30
LAM_INIT = 0.8 - 0.6 * math.exp(-0.3 * 0)
LOG2E = math.log2(math.e)

LANES = 128
BF16_SUBLANES = 16
VMEM_LIMIT = 56 * 1024 * 1024

PROJ_TILE = 512
ATTN_TQ = 512
ATTN_TK = 512
ATTN_UNROLL = 4
ROW_TILE = 512
FF_CHUNK = 1024
VT_ROWS = HEAD_WIDTH + BF16_SUBLANES

BF16 = jnp.bfloat16
F32 = jnp.float32


def _rms(x, g):
    return x * lax.rsqrt(jnp.mean(x * x, axis=-1, keepdims=True) + NORM_EPS) * g


def _rotate_half_pairs(x):
    lane = lax.broadcasted_iota(jnp.int32, x.shape, 1)
    lower = (lane % DIFF_HEAD_DIM) < (DIFF_HEAD_DIM // 2)
    return jnp.where(lower, pltpu.roll(x, LANES - 32, axis=1), pltpu.roll(x, 32, axis=1))


def _in_proj_kernel(x_ref, g_ref, w_ref, wpool_ref, pscale_ref, cos_ref, sin_ref,
                    mp_ref, q_ref, k_ref, vt_ref, ga_ref, ext_ref):
    si = pl.program_id(1)
    ts = x_ref.shape[1]
    h = _rms(x_ref[0], g_ref[...]).astype(BF16)

    def proj(lo, hi):
        return jnp.dot(h, w_ref[:, lo:hi], preferred_element_type=F32)

    gates = 1.0 / (1.0 + jnp.exp(-proj(2048, 3072)))
    ga_ref[0] = gates[:, POOL_WIDTH:].astype(BF16)

    @pl.when(si == 0)
    def _():
        ext_ref[0:POOL_HALO, :] = jnp.zeros((POOL_HALO, POOL_WIDTH), F32)

    ext_ref[POOL_HALO:, :] = proj(0, POOL_WIDTH)
    pos = si * ts + lax.broadcasted_iota(jnp.int32, (ts, 1), 0)
    for g, w in enumerate(POOL_WINDOWS):
        lo, hi = g * POOL_GROUP_WIDTH, (g + 1) * POOL_GROUP_WIDTH
        e = ext_ref[:, lo:hi]
        win = e
        shift = 1
        while shift < w:
            win = win + pltpu.roll(win, shift, axis=0)
            shift *= 2
        cnt = jnp.minimum(pos + 1, w).astype(F32)
        z = win[POOL_HALO:] / cnt - e[POOL_HALO:]
        y = jnp.dot(z.astype(BF16), wpool_ref[g], preferred_element_type=F32)
        y = y * pscale_ref[:, lo:hi] * gates[:, lo:hi]
        mp_ref[0, :, lo:hi] = y.astype(BF16)
    ext_ref[0:POOL_HALO, :] = ext_ref[ts:ts + POOL_HALO, :]

    cos = cos_ref[...]
    sin = sin_ref[...]
    qscale = DIFF_HEAD_DIM ** -0.5 * LOG2E
    q = proj(512, 1024)
    k = proj(1024, 1536)
    for c in range(Q_WIDTH // LANES):
        sl = slice(c * LANES, (c + 1) * LANES)
        qc = q[:, sl]
        kc = k[:, sl]
        q_ref[0, :, sl] = ((qc * cos + _rotate_half_pairs(qc) * sin) * qscale).astype(BF16)
        k_ref[0, :, sl] = (kc * cos + _rotate_half_pairs(kc) * sin).astype(BF16)

    v = proj(1536, 2048)
    for hd in range(DIFF_HEADS):
        vt_ref[0, hd] = v[:, hd * HEAD_WIDTH:(hd + 1) * HEAD_WIDTH].T.astype(BF16)


def _in_proj(x, g_mix, w_in, w_pool, pool_scale, cos_t, sin_t):
    B, S, D = x.shape
    ts = PROJ_TILE
    row = lambda b, s: (b, s, 0)
    const2 = lambda b, s: (0, 0)
    half = pl.BlockSpec((1, ts, 512), row)
    half_sds = jax.ShapeDtypeStruct((B, S, 512), BF16)
    vt_spec = pl.BlockSpec((1, DIFF_HEADS, HEAD_WIDTH, ts), lambda b, s: (b, 0, 0, s))
    vt_sds = jax.ShapeDtypeStruct((B, DIFF_HEADS, HEAD_WIDTH, S), BF16)
    return pl.pallas_call(
        _in_proj_kernel,
        grid=(B, S // ts),
        in_specs=[
            pl.BlockSpec((1, ts, D), row),
            pl.BlockSpec((1, D), const2),
            pl.BlockSpec((D, 3072), const2),
            pl.BlockSpec((4, 128, 128), lambda b, s: (0, 0, 0)),
            pl.BlockSpec((1, POOL_WIDTH), const2),
            pl.BlockSpec((ts, LANES), lambda b, s: (s, 0)),
            pl.BlockSpec((ts, LANES), lambda b, s: (s, 0)),
        ],
        out_specs=[half, half, half, vt_spec, half],
        out_shape=[half_sds, half_sds, half_sds, vt_sds, half_sds],
        scratch_shapes=[pltpu.VMEM((ts + POOL_HALO, POOL_WIDTH), F32)],
        compiler_params=pltpu.CompilerParams(
            dimension_semantics=("arbitrary", "arbitrary"),
            vmem_limit_bytes=VMEM_LIMIT),
        name="in_proj",
    )(x, g_mix, w_in, w_pool, pool_scale, cos_t, sin_t)


def _mem_kv_kernel(mem_ref, g_ref, w_ref, kv_ref):
    mn = _rms(mem_ref[0], g_ref[...]).astype(BF16)
    kv_ref[0] = jnp.dot(mn, w_ref[...], preferred_element_type=F32).astype(BF16)


def _mem_kv(mem, g_mem, w_ckv):
    B, M, D = mem.shape
    return pl.pallas_call(
        _mem_kv_kernel,
        grid=(B,),
        in_specs=[
            pl.BlockSpec((1, M, D), lambda b: (b, 0, 0)),
            pl.BlockSpec((1, D), lambda b: (0, 0)),
            pl.BlockSpec((D, 2 * D), lambda b: (0, 0)),
        ],
        out_specs=pl.BlockSpec((1, M, 2 * D), lambda b: (b, 0, 0)),
        out_shape=jax.ShapeDtypeStruct((B, M, 2 * D), BF16),
        compiler_params=pltpu.CompilerParams(
            dimension_semantics=("arbitrary",), vmem_limit_bytes=VMEM_LIMIT),
        name="mem_kv",
    )(mem, g_mem, w_ckv)


def _diff_attn_kernel(q_ref, k_ref, vt_ref, ga_ref, gsub_ref, lq1_ref, lk1_ref,
                      lq2_ref, lk2_ref, o_ref, qs_ref, vx_ref, m_ref, acc_ref, st0_ref):
    qi = pl.program_id(2)
    tq = q_ref.shape[1]
    tk = ATTN_TK
    nkb = vx_ref.shape[0]

    @pl.when(qi == 0)
    def _():
        for j in range(nkb):
            vx_ref[j, 0:HEAD_WIDTH, :] = vt_ref[0, 0, :, j * tk:(j + 1) * tk]
            vx_ref[j, HEAD_WIDTH:, :] = jnp.ones((BF16_SUBLANES, tk), BF16)

    q = q_ref[0]
    lane = lax.broadcasted_iota(jnp.int32, q.shape, 1)
    first = lane < DIFF_HEAD_DIM
    qs_ref[0:tq, :] = jnp.where(first, q, jnp.zeros_like(q))
    qs_ref[tq:, :] = jnp.where(first, jnp.zeros_like(q), q)
    m_ref[...] = jnp.full(m_ref.shape, NEG_BIG, F32)
    acc_ref[...] = jnp.zeros(acc_ref.shape, F32)

    def scores(kb, c):
        start = pl.multiple_of(kb * tk, tk)
        return lax.dot_general(k_ref[0, pl.ds(start, tk), :], qs_ref[c * tq:(c + 1) * tq, :],
                               (((1,), (1,)), ((), ())),
                               preferred_element_type=F32)

    def softmax(st, c, masked):
        cols = slice(c * tq, (c + 1) * tq)
        if masked:
            key = lax.broadcasted_iota(jnp.int32, st.shape, 0)
            qry = lax.broadcasted_iota(jnp.int32, st.shape, 1)
            st = jnp.where(key <= qry, st, NEG_BIG)
        m_prev = m_ref[:, cols]
        m_new = jnp.maximum(m_prev, jnp.max(st, axis=0, keepdims=True))
        m_ref[:, cols] = m_new
        return jnp.exp2(m_prev - m_new), jnp.exp2(st - m_new).astype(BF16)

    def pv(kb, c, alpha, pt):
        cols = slice(c * tq, (c + 1) * tq)
        acc_ref[:, cols] = alpha * acc_ref[:, cols] + jnp.dot(
            vx_ref[kb], pt, preferred_element_type=F32)

    def block(kb, masked):
        st1 = scores(kb, 1)
        a0, p0 = softmax(st0_ref[...], 0, masked)
        pv(kb, 0, a0, p0)
        if not masked:
            st0_ref[...] = scores(kb + 1, 0)
        a1, p1 = softmax(st1, 1, masked)
        pv(kb, 1, a1, p1)

    st0_ref[...] = scores(0, 0)

    def body(u, carry):
        for j in range(ATTN_UNROLL):
            block(ATTN_UNROLL * u + j, False)
        return carry

    def single(kb, carry):
        block(kb, False)
        return carry

    nfull = qi // ATTN_UNROLL
    lax.fori_loop(0, nfull, body, 0)
    lax.fori_loop(nfull * ATTN_UNROLL, qi, single, 0)
    block(qi, True)

    lam = (jnp.exp(jnp.sum(lq1_ref[...] * lk1_ref[...], keepdims=True))
           - jnp.exp(jnp.sum(lq2_ref[...] * lk2_ref[...], keepdims=True)) + LAM_INIT)
    o1 = acc_ref[0:HEAD_WIDTH, 0:tq] / acc_ref[HEAD_WIDTH:HEAD_WIDTH + 1, 0:tq]
    o2 = acc_ref[0:HEAD_WIDTH, tq:] / acc_ref[HEAD_WIDTH:HEAD_WIDTH + 1, tq:]
    ot = o1 - lam * o2
    inv = lax.rsqrt(jnp.mean(ot * ot, axis=0, keepdims=True) + NORM_EPS)
    yt = ot * inv * gsub_ref[...] * (1.0 - LAM_INIT)
    o_ref[0] = (yt.T * ga_ref[0].astype(F32)).astype(BF16)


def _diff_attn(q, k, vt, ga, g_subln_col, lq1, lk1, lq2, lk2):
    B, S, _ = q.shape
    tq = ATTN_TQ
    assert tq == ATTN_TK
    qblk = pl.BlockSpec((1, tq, HEAD_WIDTH), lambda b, h, i: (b, i, h))
    kblk = pl.BlockSpec((1, S, HEAD_WIDTH), lambda b, h, i: (b, 0, h))
    vtblk = pl.BlockSpec((1, 1, HEAD_WIDTH, S), lambda b, h, i: (b, h, 0, 0))
    vec = lambda n: pl.BlockSpec((1, n), lambda b, h, i: (0, 0))
    return pl.pallas_call(
        _diff_attn_kernel,
        grid=(B, DIFF_HEADS, S // tq),
        in_specs=[qblk, kblk, vtblk, qblk,
                  pl.BlockSpec((HEAD_WIDTH, 1), lambda b, h, i: (0, 0)),
                  vec(DIFF_HEAD_DIM), vec(DIFF_HEAD_DIM), vec(DIFF_HEAD_DIM), vec(DIFF_HEAD_DIM)],
        out_specs=qblk,
        out_shape=jax.ShapeDtypeStruct((B, S, ATTN_WIDTH), BF16),
        scratch_shapes=[
            pltpu.VMEM((2 * tq, HEAD_WIDTH), BF16),
            pltpu.VMEM((S // ATTN_TK, VT_ROWS, ATTN_TK), BF16),
            pltpu.VMEM((1, 2 * tq), F32),
            pltpu.VMEM((VT_ROWS, 2 * tq), F32),
            pltpu.VMEM((ATTN_TK, tq), F32),
        ],
        compiler_params=pltpu.CompilerParams(
            dimension_semantics=("arbitrary", "arbitrary", "arbitrary"),
            vmem_limit_bytes=VMEM_LIMIT),
        name="diff_attn",
    )(q, k, vt, ga, g_subln_col, lq1, lk1, lq2, lk2)


def _mix_cross_kernel(x_ref, mp_ref, ya_ref, wout_ref, gc_ref, wcq_ref, kv_ref, wco_ref,
                      o_ref, oc_ref):
    x1 = (x_ref[0]
          + jnp.dot(mp_ref[0], wout_ref[0:POOL_WIDTH, :], preferred_element_type=F32)
          + jnp.dot(ya_ref[0], wout_ref[POOL_WIDTH:, :], preferred_element_type=F32))
    hc = _rms(x1, gc_ref[...]).astype(BF16)
    qc = (jnp.dot(hc, wcq_ref[...], preferred_element_type=F32)
          * CROSS_HEAD_DIM ** -0.5).astype(BF16)
    for h in range(CROSS_HEADS):
        sl = slice(h * CROSS_HEAD_DIM, (h + 1) * CROSS_HEAD_DIM)
        kh = kv_ref[0, :, sl]
        vh = kv_ref[0, :, D_MODEL + h * CROSS_HEAD_DIM:D_MODEL + (h + 1) * CROSS_HEAD_DIM]
        s = lax.dot_general(qc[:, sl], kh, (((1,), (1,)), ((), ())),
                            preferred_element_type=F32)
        p = jnp.exp(s - jnp.max(s, axis=1, keepdims=True))
        l = jnp.sum(p, axis=1, keepdims=True)
        oh = jnp.dot(p.astype(BF16), vh, preferred_element_type=F32) / l
        oc_ref[:, sl] = oh.astype(BF16)
    o_ref[0] = x1 + jnp.dot(oc_ref[...], wco_ref[...], preferred_element_type=F32)


def _mix_cross(x, mp, ya, w_out, g_cross, w_cq, kv, w_co):
    B, S, D = x.shape
    ts = ROW_TILE
    M = kv.shape[1]
    row = lambda b, s: (b, s, 0)
    const2 = lambda b, s: (0, 0)
    wspec = pl.BlockSpec((D, D), const2)
    return pl.pallas_call(
        _mix_cross_kernel,
        grid=(B, S // ts),
        in_specs=[
            pl.BlockSpec((1, ts, D), row),
            pl.BlockSpec((1, ts, POOL_WIDTH), row),
            pl.BlockSpec((1, ts, ATTN_WIDTH), row),
            wspec,
            pl.BlockSpec((1, D), const2),
            wspec,
            pl.BlockSpec((1, M, 2 * D), lambda b, s: (b, 0, 0)),
            wspec,
        ],
        out_specs=pl.BlockSpec((1, ts, D), row),
        out_shape=jax.ShapeDtypeStruct((B, S, D), F32),
        scratch_shapes=[pltpu.VMEM((ts, D), BF16)],
        compiler_params=pltpu.CompilerParams(
            dimension_semantics=("arbitrary", "arbitrary"),
            vmem_limit_bytes=VMEM_LIMIT),
        name="mix_cross",
    )(x, mp, ya, w_out, g_cross, w_cq, kv, w_co)


def _mlp_kernel(x_ref, gm_ref, wup_ref, wdn_ref, gf_ref, o_ref):
    x2 = x_ref[0]
    hm = _rms(x2, gm_ref[...]).astype(BF16)
    acc = x2
    for c in range(D_FF // FF_CHUNK):
        sl = slice(c * FF_CHUNK, (c + 1) * FF_CHUNK)
        up = jnp.dot(hm, wup_ref[:, sl], preferred_element_type=F32)
        a = jnp.square(jnp.maximum(up, 0.0)).astype(BF16)
        acc = acc + jnp.dot(a, wdn_ref[sl, :], preferred_element_type=F32)
    o_ref[0] = _rms(acc, gf_ref[...])


def _mlp(x2, g_mlp, w_up, w_down, g_final):
    B, S, D = x2.shape
    ts = ROW_TILE
    row = lambda b, s: (b, s, 0)
    const2 = lambda b, s: (0, 0)
    return pl.pallas_call(
        _mlp_kernel,
        grid=(B, S // ts),
        in_specs=[
            pl.BlockSpec((1, ts, D), row),
            pl.BlockSpec((1, D), const2),
            pl.BlockSpec((D, D_FF), const2),
            pl.BlockSpec((D_FF, D), const2),
            pl.BlockSpec((1, D), const2),
        ],
        out_specs=pl.BlockSpec((1, ts, D), row),
        out_shape=jax.ShapeDtypeStruct((B, S, D), F32),
        compiler_params=pltpu.CompilerParams(
            dimension_semantics=("arbitrary", "arbitrary"),
            vmem_limit_bytes=VMEM_LIMIT),
        name="mlp",
    )(x2, g_mlp, w_up, w_down, g_final)


def _rope_tables(S):
    dh = DIFF_HEAD_DIM
    inv_freq = ROPE_THETA ** (-jnp.arange(0, dh, 2, dtype=F32) / dh)
    assert S % LANES == 0
    ang_hi = (jnp.arange(S // LANES, dtype=F32) * LANES)[:, None] * inv_freq[None, :]
    ang_lo = jnp.arange(LANES, dtype=F32)[:, None] * inv_freq[None, :]
    ch, sh = jnp.cos(ang_hi)[:, None, :], jnp.sin(ang_hi)[:, None, :]
    cl, sl = jnp.cos(ang_lo)[None, :, :], jnp.sin(ang_lo)[None, :, :]
    cos = (ch * cl - sh * sl).reshape(S, dh // 2)
    sin = (sh * cl + ch * sl).reshape(S, dh // 2)
    cos_t = jnp.concatenate([cos, cos, cos, cos], axis=-1)
    sin_t = jnp.concatenate([-sin, sin, -sin, sin], axis=-1)
    return cos_t, sin_t


def kernel(x, mem, g_mix, w_in, w_pool, pool_scale, lambda_q1, lambda_k1, lambda_q2,
           lambda_k2, g_subln, w_out, g_cross, g_mem, w_cq, w_ckv, w_co, g_mlp, w_up,
           w_down, g_final):
    B, S, _ = x.shape
    assert g_mix.shape[0] == 1, "single-layer block"
    cos_t, sin_t = _rope_tables(S)
    mp, q, k, vt, ga = _in_proj(x, g_mix, w_in[0].astype(BF16), w_pool[0].astype(BF16),
                                pool_scale, cos_t, sin_t)
    kv = _mem_kv(mem, g_mem, w_ckv[0].astype(BF16))
    ya = _diff_attn(q, k, vt, ga, g_subln.reshape(HEAD_WIDTH, 1), lambda_q1, lambda_k1,
                    lambda_q2, lambda_k2)
    x2 = _mix_cross(x, mp, ya, w_out[0].astype(BF16), g_cross, w_cq[0].astype(BF16), kv,
                    w_co[0].astype(BF16))
    return _mlp(x2, g_mlp, w_up[0].astype(BF16), w_down[0].astype(BF16),
                g_final.reshape(1, D_MODEL))
```

```python
import math

import jax
import jax.numpy as jnp
from jax import lax
from jax.experimental import pallas as pl
from jax.experimental.pallas import tpu as pltpu

D_MODEL = 1024
POOL_WIDTH = 512
POOL_WINDOWS = (2, 4, 8, 16)
POOL_GROUP_WIDTH = 128
POOL_HALO = 16
DIFF_HEADS = 4
DIFF_HEAD_DIM = 64
HEAD_WIDTH = 2 * DIFF_HEAD_DIM
Q_WIDTH = 512
ATTN_WIDTH = 512
ROPE_THETA = 10000.0
CROSS_HEADS = 4
CROSS_HEAD_DIM = 256
D_FF = 4096
NORM_EPS = 1e-6
NEG_BIG = -1e30
LAM_INIT = 0.8 - 0.6 * math.exp(-0.3 * 0)
LOG2E = math.log2(math.e)

LANES = 128
BF16_SUBLANES = 16
VMEM_LIMIT = 56 * 1024 * 1024

PROJ_TILE = 512
ATTN_TQ = 512
ATTN_TK = 512
ATTN_UNROLL = 4
ROW_TILE = 512
FF_CHUNK = 1024
VT_ROWS = HEAD_WIDTH + BF16_SUBLANES

BF16 = jnp.bfloat16
F32 = jnp.float32


def _rms(x, g):
    return x * lax.rsqrt(jnp.mean(x * x, axis=-1, keepdims=True) + NORM_EPS) * g


def _rotate_half_pairs(x):
    lane = lax.broadcasted_iota(jnp.int32, x.shape, 1)
    lower = (lane % DIFF_HEAD_DIM) < (DIFF_HEAD_DIM // 2)
    return jnp.where(lower, pltpu.roll(x, LANES - 32, axis=1), pltpu.roll(x, 32, axis=1))


def _in_proj_kernel(x_ref, g_ref, w_ref, wpool_ref, pscale_ref, cos_ref, sin_ref,
                    mp_ref, q_ref, k_ref, vt_ref, ga_ref, ext_ref):
    si = pl.program_id(1)
    ts = x_ref.shape[1]
    h = _rms(x_ref[0], g_ref[...]).astype(BF16)

    def proj(lo, hi):
        return jnp.dot(h, w_ref[:, lo:hi], preferred_element_type=F32)

    gates = 1.0 / (1.0 + jnp.exp(-proj(2048, 3072)))
    ga_ref[0] = gates[:, POOL_WIDTH:].astype(BF16)

    @pl.when(si == 0)
    def _():
        ext_ref[0:POOL_HALO, :] = jnp.zeros((POOL_HALO, POOL_WIDTH), F32)

    ext_ref[POOL_HALO:, :] = proj(0, POOL_WIDTH)
    pos = si * ts + lax.broadcasted_iota(jnp.int32, (ts, 1), 0)
    for g, w in enumerate(POOL_WINDOWS):
        lo, hi = g * POOL_GROUP_WIDTH, (g + 1) * POOL_GROUP_WIDTH
        e = ext_ref[:, lo:hi]
        win = e
        shift = 1
        while shift < w:
            win = win + pltpu.roll(win, shift, axis=0)
            shift *= 2
        cnt = jnp.minimum(pos + 1, w).astype(F32)
        z = win[POOL_HALO:] / cnt - e[POOL_HALO:]
        y = jnp.dot(z.astype(BF16), wpool_ref[g], preferred_element_type=F32)
        y = y * pscale_ref[:, lo:hi] * gates[:, lo:hi]
        mp_ref[0, :, lo:hi] = y.astype(BF16)
    ext_ref[0:POOL_HALO, :] = ext_ref[ts:ts + POOL_HALO, :]

    cos = cos_ref[...]
    sin = sin_ref[...]
    qscale = DIFF_HEAD_DIM ** -0.5 * LOG2E
    q = proj(512, 1024)
    k = proj(1024, 1536)
    for c in range(Q_WIDTH // LANES):
        sl = slice(c * LANES, (c + 1) * LANES)
        qc = q[:, sl]
        kc = k[:, sl]
        q_ref[0, :, sl] = ((qc * cos + _rotate_half_pairs(qc) * sin) * qscale).astype(BF16)
        k_ref[0, :, sl] = (kc * cos + _rotate_half_pairs(kc) * sin).astype(BF16)

    v = proj(1536, 2048)
    for hd in range(DIFF_HEADS):
        vt_ref[0, hd] = v[:, hd * HEAD_WIDTH:(hd + 1) * HEAD_WIDTH].T.astype(BF16)


def _in_proj(x, g_mix, w_in, w_pool, pool_scale, cos_t, sin_t):
    B, S, D = x.shape
    ts = PROJ_TILE
    row = lambda b, s: (b, s, 0)
    const2 = lambda b, s: (0, 0)
    half = pl.BlockSpec((1, ts, 512), row)
    half_sds = jax.ShapeDtypeStruct((B, S, 512), BF16)
    vt_spec = pl.BlockSpec((1, DIFF_HEADS, HEAD_WIDTH, ts), lambda b, s: (b, 0, 0, s))
    vt_sds = jax.ShapeDtypeStruct((B, DIFF_HEADS, HEAD_WIDTH, S), BF16)
    return pl.pallas_call(
        _in_proj_kernel,
        grid=(B, S // ts),
        in_specs=[
            pl.BlockSpec((1, ts, D), row),
            pl.BlockSpec((1, D), const2),
            pl.BlockSpec((D, 3072), const2),
            pl.BlockSpec((4, 128, 128), lambda b, s: (0, 0, 0)),
            pl.BlockSpec((1, POOL_WIDTH), const2),
            pl.BlockSpec((ts, LANES), lambda b, s: (s, 0)),
            pl.BlockSpec((ts, LANES), lambda b, s: (s, 0)),
        ],
        out_specs=[half, half, half, vt_spec, half],
        out_shape=[half_sds, half_sds, half_sds, vt_sds, half_sds],
        scratch_shapes=[pltpu.VMEM((ts + POOL_HALO, POOL_WIDTH), F32)],
        compiler_params=pltpu.CompilerParams(
            dimension_semantics=("arbitrary", "arbitrary"),
            vmem_limit_bytes=VMEM_LIMIT),
        name="in_proj",
    )(x, g_mix, w_in, w_pool, pool_scale, cos_t, sin_t)


def _mem_kv_kernel(mem_ref, g_ref, w_ref, kv_ref):
    mn = _rms(mem_ref[0], g_ref[...]).astype(BF16)
    kv_ref[0] = jnp.dot(mn, w_ref[...], preferred_element_type=F32).astype(BF16)


def _mem_kv(mem, g_mem, w_ckv):
    B, M, D = mem.shape
    return pl.pallas_call(
        _mem_kv_kernel,
        grid=(B,),
        in_specs=[
            pl.BlockSpec((1, M, D), lambda b: (b, 0, 0)),
            pl.BlockSpec((1, D), lambda b: (0, 0)),
            pl.BlockSpec((D, 2 * D), lambda b: (0, 0)),
        ],
        out_specs=pl.BlockSpec((1, M, 2 * D), lambda b: (b, 0, 0)),
        out_shape=jax.ShapeDtypeStruct((B, M, 2 * D), BF16),
        compiler_params=pltpu.CompilerParams(
            dimension_semantics=("arbitrary",), vmem_limit_bytes=VMEM_LIMIT),
        name="mem_kv",
    )(mem, g_mem, w_ckv)


def _diff_attn_kernel(q_ref, k_ref, vt_ref, ga_ref, gsub_ref, lq1_ref, lk1_ref,
                      lq2_ref, lk2_ref, o_ref, qs_ref, vx_ref, m_ref, acc_ref, st0_ref):
    qi = pl.program_id(2)
    tq = q_ref.shape[1]
    tk = ATTN_TK
    nkb = vx_ref.shape[0]

    @pl.when(qi == 0)
    def _():
        for j in range(nkb):
            vx_ref[j, 0:HEAD_WIDTH, :] = vt_ref[0, 0, :, j * tk:(j + 1) * tk]
            vx_ref[j, HEAD_WIDTH:, :] = jnp.ones((BF16_SUBLANES, tk), BF16)

    qt = q_ref[0].astype(F32).T
    feat = lax.broadcasted_iota(jnp.int32, qt.shape, 0)
    first = feat < DIFF_HEAD_DIM
    qs_ref[:, 0:tq] = jnp.where(first, qt, 0.0).astype(BF16)
    qs_ref[:, tq:] = jnp.where(first, 0.0, qt).astype(BF16)
    m_ref[...] = jnp.full(m_ref.shape, NEG_BIG, F32)
    acc_ref[...] = jnp.zeros(acc_ref.shape, F32)

    def scores(kb, c):
        start = pl.multiple_of(kb * tk, tk)
        return jnp.dot(k_ref[0, pl.ds(start, tk), :], qs_ref[:, c * tq:(c + 1) * tq],
                       preferred_element_type=F32)

    def softmax(st, c, masked):
        cols = slice(c * tq, (c + 1) * tq)
        if masked:
            key = lax.broadcasted_iota(jnp.int32, st.shape, 0)
            qry = lax.broadcasted_iota(jnp.int32, st.shape, 1)
            st = jnp.where(key <= qry, st, NEG_BIG)
        m_prev = m_ref[:, cols]
        m_new = jnp.maximum(m_prev, jnp.max(st, axis=0, keepdims=True))
        m_ref[:, cols] = m_new
        return jnp.exp2(m_prev - m_new), jnp.exp2(st - m_new).astype(BF16)

    def pv(kb, c, alpha, pt):
        cols = slice(c * tq, (c + 1) * tq)
        acc_ref[:, cols] = alpha * acc_ref[:, cols] + jnp.dot(
            vx_ref[kb], pt, preferred_element_type=F32)

    def block(kb, masked):
        st1 = scores(kb, 1)
        a0, p0 = softmax(st0_ref[...], 0, masked)
        pv(kb, 0, a0, p0)
        if not masked:
            st0_ref[...] = scores(kb + 1, 0)
        a1, p1 = softmax(st1, 1, masked)
        pv(kb, 1, a1, p1)

    st0_ref[...] = scores(0, 0)

    def body(u, carry):
        for j in range(ATTN_UNROLL):
            block(ATTN_UNROLL * u + j, False)
        return carry

    nfull = qi // ATTN_UNROLL
    lax.fori_loop(0, nfull, body, 0)

    for r in range(ATTN_UNROLL):
        @pl.when(qi % ATTN_UNROLL == r)
        def _(r=r):
            for j in range(r):
                block(nfull * ATTN_UNROLL + j, False)
            block(qi, True)

    lam = (jnp.exp(jnp.sum(lq1_ref[...] * lk1_ref[...], keepdims=True))
           - jnp.exp(jnp.sum(lq2_ref[...] * lk2_ref[...], keepdims=True)) + LAM_INIT)
    o1 = acc_ref[0:HEAD_WIDTH, 0:tq] / acc_ref[HEAD_WIDTH:HEAD_WIDTH + 1, 0:tq]
    o2 = acc_ref[0:HEAD_WIDTH, tq:] / acc_ref[HEAD_WIDTH:HEAD_WIDTH + 1, tq:]
    ot = o1 - lam * o2
    inv = lax.rsqrt(jnp.mean(ot * ot, axis=0, keepdims=True) + NORM_EPS)
    yt = ot * inv * gsub_ref[...] * (1.0 - LAM_INIT)
    o_ref[0] = (yt.T * ga_ref[0].astype(F32)).astype(BF16)


def _diff_attn(q, k, vt, ga, g_subln_col, lq1, lk1, lq2, lk2):
    B, S, _ = q.shape
    tq = ATTN_TQ
    assert tq == ATTN_TK
    qblk = pl.BlockSpec((1, tq, HEAD_WIDTH), lambda b, h, i: (b, i, h))
    kblk = pl.BlockSpec((1, S, HEAD_WIDTH), lambda b, h, i: (b, 0, h))
    vtblk = pl.BlockSpec((1, 1, HEAD_WIDTH, S), lambda b, h, i: (b, h, 0, 0))
    vec = lambda n: pl.BlockSpec((1, n), lambda b, h, i: (0, 0))
    return pl.pallas_call(
        _diff_attn_kernel,
        grid=(B, DIFF_HEADS, S // tq),
        in_specs=[qblk, kblk, vtblk, qblk,
                  pl.BlockSpec((HEAD_WIDTH, 1), lambda b, h, i: (0, 0)),
                  vec(DIFF_HEAD_DIM), vec(DIFF_HEAD_DIM), vec(DIFF_HEAD_DIM), vec(DIFF_HEAD_DIM)],
        out_specs=qblk,
        out_shape=jax.ShapeDtypeStruct((B, S, ATTN_WIDTH), BF16),
        scratch_shapes=[
            pltpu.VMEM((HEAD_WIDTH, 2 * tq), BF16),
            pltpu.VMEM((S // ATTN_TK, VT_ROWS, ATTN_TK), BF16),
            pltpu.VMEM((1, 2 * tq), F32),
            pltpu.VMEM((VT_ROWS, 2 * tq), F32),
            pltpu.VMEM((ATTN_TK, tq), F32),
        ],
        compiler_params=pltpu.CompilerParams(
            dimension_semantics=("arbitrary", "arbitrary", "arbitrary"),
            vmem_limit_bytes=VMEM_LIMIT),
        name="diff_attn",
    )(q, k, vt, ga, g_subln_col, lq1, lk1, lq2, lk2)


def _mix_cross_kernel(x_ref, mp_ref, ya_ref, wout_ref, gc_ref, wcq_ref, kv_ref, wco_ref,
                      o_ref, oc_ref):
    x1 = (x_ref[0]
          + jnp.dot(mp_ref[0], wout_ref[0:POOL_WIDTH, :], preferred_element_type=F32)
          + jnp.dot(ya_ref[0], wout_ref[POOL_WIDTH:, :], preferred_element_type=F32))
    hc = _rms(x1, gc_ref[...]).astype(BF16)
    qc = (jnp.dot(hc, wcq_ref[...], preferred_element_type=F32)
          * CROSS_HEAD_DIM ** -0.5).astype(BF16)
    for h in range(CROSS_HEADS):
        sl = slice(h * CROSS_HEAD_DIM, (h + 1) * CROSS_HEAD_DIM)
        kh = kv_ref[0, :, sl]
        vh = kv_ref[0, :, D_MODEL + h * CROSS_HEAD_DIM:D_MODEL + (h + 1) * CROSS_HEAD_DIM]
        s = lax.dot_general(qc[:, sl], kh, (((1,), (1,)), ((), ())),
                            preferred_element_type=F32)
        p = jnp.exp(s - jnp.max(s, axis=1, keepdims=True))
        l = jnp.sum(p, axis=1, keepdims=True)
        oh = jnp.dot(p.astype(BF16), vh, preferred_element_type=F32) / l
        oc_ref[:, sl] = oh.astype(BF16)
    o_ref[0] = x1 + jnp.dot(oc_ref[...], wco_ref[...], preferred_element_type=F32)


def _mix_cross(x, mp, ya, w_out, g_cross, w_cq, kv, w_co):
    B, S, D = x.shape
    ts = ROW_TILE
    M = kv.shape[1]
    row = lambda b, s: (b, s, 0)
    const2 = lambda b, s: (0, 0)
    wspec = pl.BlockSpec((D, D), const2)
    return pl.pallas_call(
        _mix_cross_kernel,
        grid=(B, S // ts),
        in_specs=[
            pl.BlockSpec((1, ts, D), row),
            pl.BlockSpec((1, ts, POOL_WIDTH), row),
            pl.BlockSpec((1, ts, ATTN_WIDTH), row),
            wspec,
            pl.BlockSpec((1, D), const2),
            wspec,
            pl.BlockSpec((1, M, 2 * D), lambda b, s: (b, 0, 0)),
            wspec,
        ],
        out_specs=pl.BlockSpec((1, ts, D), row),
        out_shape=jax.ShapeDtypeStruct((B, S, D), F32),
        scratch_shapes=[pltpu.VMEM((ts, D), BF16)],
        compiler_params=pltpu.CompilerParams(
            dimension_semantics=("arbitrary", "arbitrary"),
            vmem_limit_bytes=VMEM_LIMIT),
        name="mix_cross",
    )(x, mp, ya, w_out, g_cross, w_cq, kv, w_co)


def _mlp_kernel(x_ref, gm_ref, wup_ref, wdn_ref, gf_ref, o_ref):
    x2 = x_ref[0]
    hm = _rms(x2, gm_ref[...]).astype(BF16)
    acc = x2
    for c in range(D_FF // FF_CHUNK):
        sl = slice(c * FF_CHUNK, (c + 1) * FF_CHUNK)
        up = jnp.dot(hm, wup_ref[:, sl], preferred_element_type=F32)
        a = jnp.square(jnp.maximum(up, 0.0)).astype(BF16)
        acc = acc + jnp.dot(a, wdn_ref[sl, :], preferred_element_type=F32)
    o_ref[0] = _rms(acc, gf_ref[...])


def _mlp(x2, g_mlp, w_up, w_down, g_final):
    B, S, D = x2.shape
    ts = ROW_TILE
    row = lambda b, s: (b, s, 0)
    const2 = lambda b, s: (0, 0)
    return pl.pallas_call(
        _mlp_kernel,
        grid=(B, S // ts),
        in_specs=[
            pl.BlockSpec((1, ts, D), row),
            pl.BlockSpec((1, D), const2),
            pl.BlockSpec((D, D_FF), const2),
            pl.BlockSpec((D_FF, D), const2),
            pl.BlockSpec((1, D), const2),
        ],
        out_specs=pl.BlockSpec((1, ts, D), row),
        out_shape=jax.ShapeDtypeStruct((B, S, D), F32),
        compiler_params=pltpu.CompilerParams(
            dimension_semantics=("arbitrary", "arbitrary"),
            vmem_limit_bytes=VMEM_LIMIT),
        name="mlp",
    )(x2, g_mlp, w_up, w_down, g_final)


def _rope_tables(S):
    dh = DIFF_HEAD_DIM
    inv_freq = ROPE_THETA ** (-jnp.arange(0, dh, 2, dtype=F32) / dh)
    inv_freq = jnp.tile(inv_freq, LANES // (dh // 2))
    sign = jnp.where((jnp.arange(LANES) % dh) < dh // 2, -1.0, 1.0).astype(F32)
    assert S % LANES == 0
    ang_hi = (jnp.arange(S // LANES, dtype=F32) * LANES)[:, None] * inv_freq[None, :]
    ang_lo = jnp.arange(LANES, dtype=F32)[:, None] * inv_freq[None, :]
    ch, sh, cl, sl = lax.optimization_barrier(
        (jnp.cos(ang_hi), jnp.sin(ang_hi), jnp.cos(ang_lo), jnp.sin(ang_lo)))
    ch, sh, cl, sl = ch[:, None, :], sh[:, None, :], cl[None, :, :], sl[None, :, :]
    cos_t = (ch * cl - sh * sl).reshape(S, LANES)
    sin_t = ((sh * cl + ch * sl) * sign).reshape(S, LANES)
    return cos_t, sin_t


def kernel(x, mem, g_mix, w_in, w_pool, pool_scale, lambda_q1, lambda_k1, lambda_q2,
           lambda_k2, g_subln, w_out, g_cross, g_mem, w_cq, w_ckv, w_co, g_mlp, w_up,
           w_down, g_final):
    B, S, _ = x.shape
    assert g_mix.shape[0] == 1, "single-layer block"
    cos_t, sin_t = _rope_tables(S)
    mp, q, k, vt, ga = _in_proj(x, g_mix, w_in[0].astype(BF16), w_pool[0].astype(BF16),
                                pool_scale, cos_t, sin_t)
    kv = _mem_kv(mem, g_mem, w_ckv[0].astype(BF16))
    ya = _diff_attn(q, k, vt, ga, g_subln.reshape(HEAD_WIDTH, 1), lambda_q1, lambda_k1,
                    lambda_q2, lambda_k2)
    x2 = _mix_cross(x, mp, ya, w_out[0].astype(BF16), g_cross, w_cq[0].astype(BF16), kv,
                    w_co[0].astype(BF16))
    return _mlp(x2, g_mlp, w_up[0].astype(BF16), w_down[0].astype(BF16),
                g_final.reshape(1, D_MODEL))
```

```python
import math

import jax
import jax.numpy as jnp
from jax import lax
from jax.experimental import pallas as pl
from jax.experimental.pallas import tpu as pltpu

D_MODEL = 1024
POOL_WIDTH = 512
POOL_WINDOWS = (2, 4, 8, 16)
POOL_GROUP_WIDTH = 128
POOL_HALO = 16
DIFF_HEADS = 4
DIFF_HEAD_DIM = 64
HEAD_WIDTH = 2 * DIFF_HEAD_DIM
Q_WIDTH = 512
ATTN_WIDTH = 512
ROPE_THETA = 10000.0
CROSS_HEADS = 4
CROSS_HEAD_DIM = 256
D_FF = 4096
NORM_EPS = 1e-6
NEG_BIG = -1e30
LAM_INIT = 0.8 - 0.6 * math.exp(-0.3 * 0)
LOG2E = math.log2(math.e)

LANES = 128
BF16_SUBLANES = 16
VMEM_LIMIT = 60 * 1024 * 1024

PROJ_TILE = 1024
ATTN_TQ = 512
ATTN_TK = 512
ATTN_UNROLL = 4
ROW_TILE = 1024
FF_CHUNK = 1024
VT_ROWS = HEAD_WIDTH + BF16_SUBLANES

BF16 = jnp.bfloat16
F32 = jnp.float32


def _rms(x, g):
    return x * lax.rsqrt(jnp.mean(x * x, axis=-1, keepdims=True) + NORM_EPS) * g


def _rotate_half_pairs(x):
    lane = lax.broadcasted_iota(jnp.int32, x.shape, 1)
    lower = (lane % DIFF_HEAD_DIM) < (DIFF_HEAD_DIM // 2)
    return jnp.where(lower, pltpu.roll(x, LANES - 32, axis=1), pltpu.roll(x, 32, axis=1))


def _in_proj_kernel(x_ref, g_ref, w_ref, wpool_ref, pscale_ref, cos_ref, sin_ref,
                    mp_ref, q_ref, k_ref, vt_ref, ga_ref, ext_ref):
    si = pl.program_id(1)
    ts = x_ref.shape[1]
    h = _rms(x_ref[0], g_ref[...]).astype(BF16)

    def proj(lo, hi):
        return jnp.dot(h, w_ref[:, lo:hi], preferred_element_type=F32)

    gates = 1.0 / (1.0 + jnp.exp(-proj(2048, 3072)))
    ga_ref[0] = gates[:, POOL_WIDTH:].astype(BF16)

    @pl.when(si == 0)
    def _():
        ext_ref[0:POOL_HALO, :] = jnp.zeros((POOL_HALO, POOL_WIDTH), F32)

    ext_ref[POOL_HALO:, :] = proj(0, POOL_WIDTH)
    q = proj(512, 1024)
    k = proj(1024, 1536)
    v = proj(1536, 2048)

    pos = si * ts + lax.broadcasted_iota(jnp.int32, (ts, 1), 0)
    zs = []
    for g, w in enumerate(POOL_WINDOWS):
        lo, hi = g * POOL_GROUP_WIDTH, (g + 1) * POOL_GROUP_WIDTH
        e = ext_ref[:, lo:hi]
        win = e
        shift = 1
        while shift < w:
            win = win + pltpu.roll(win, shift, axis=0)
            shift *= 2
        cnt = jnp.minimum(pos + 1, w).astype(F32)
        zs.append((win[POOL_HALO:] / cnt - e[POOL_HALO:]).astype(BF16))
    ext_ref[0:POOL_HALO, :] = ext_ref[ts:ts + POOL_HALO, :]

    cos = cos_ref[...]
    sin = sin_ref[...]
    qscale = DIFF_HEAD_DIM ** -0.5 * LOG2E
    for c in range(Q_WIDTH // LANES):
        sl = slice(c * LANES, (c + 1) * LANES)
        qc = q[:, sl]
        kc = k[:, sl]
        q_ref[0, :, sl] = ((qc * cos + _rotate_half_pairs(qc) * sin) * qscale).astype(BF16)
        k_ref[0, :, sl] = (kc * cos + _rotate_half_pairs(kc) * sin).astype(BF16)

    for hd in range(DIFF_HEADS):
        vt_ref[0, hd] = v[:, hd * HEAD_WIDTH:(hd + 1) * HEAD_WIDTH].T.astype(BF16)

    for g in range(len(POOL_WINDOWS)):
        lo, hi = g * POOL_GROUP_WIDTH, (g + 1) * POOL_GROUP_WIDTH
        y = jnp.dot(zs[g], wpool_ref[g], preferred_element_type=F32)
        y = y * pscale_ref[:, lo:hi] * gates[:, lo:hi]
        mp_ref[0, :, lo:hi] = y.astype(BF16)


def _in_proj(x, g_mix, w_in, w_pool, pool_scale, cos_t, sin_t):
    B, S, D = x.shape
    ts = PROJ_TILE
    row = lambda b, s: (b, s, 0)
    const2 = lambda b, s: (0, 0)
    half = pl.BlockSpec((1, ts, 512), row)
    half_sds = jax.ShapeDtypeStruct((B, S, 512), BF16)
    vt_spec = pl.BlockSpec((1, DIFF_HEADS, HEAD_WIDTH, ts), lambda b, s: (b, 0, 0, s))
    vt_sds = jax.ShapeDtypeStruct((B, DIFF_HEADS, HEAD_WIDTH, S), BF16)
    return pl.pallas_call(
        _in_proj_kernel,
        grid=(B, S // ts),
        in_specs=[
            pl.BlockSpec((1, ts, D), row),
            pl.BlockSpec((1, D), const2),
            pl.BlockSpec((D, 3072), const2),
            pl.BlockSpec((4, 128, 128), lambda b, s: (0, 0, 0)),
            pl.BlockSpec((1, POOL_WIDTH), const2),
            pl.BlockSpec((ts, LANES), lambda b, s: (s, 0)),
            pl.BlockSpec((ts, LANES), lambda b, s: (s, 0)),
        ],
        out_specs=[half, half, half, vt_spec, half],
        out_shape=[half_sds, half_sds, half_sds, vt_sds, half_sds],
        scratch_shapes=[pltpu.VMEM((ts + POOL_HALO, POOL_WIDTH), F32)],
        compiler_params=pltpu.CompilerParams(
            dimension_semantics=("arbitrary", "arbitrary"),
            vmem_limit_bytes=VMEM_LIMIT),
        name="in_proj",
    )(x, g_mix, w_in, w_pool, pool_scale, cos_t, sin_t)


def _mem_kv_kernel(mem_ref, g_ref, w_ref, kv_ref):
    mn = _rms(mem_ref[0], g_ref[...]).astype(BF16)
    kv_ref[0] = jnp.dot(mn, w_ref[...], preferred_element_type=F32).astype(BF16)


def _mem_kv(mem, g_mem, w_ckv):
    B, M, D = mem.shape
    return pl.pallas_call(
        _mem_kv_kernel,
        grid=(B,),
        in_specs=[
            pl.BlockSpec((1, M, D), lambda b: (b, 0, 0)),
            pl.BlockSpec((1, D), lambda b: (0, 0)),
            pl.BlockSpec((D, 2 * D), lambda b: (0, 0)),
        ],
        out_specs=pl.BlockSpec((1, M, 2 * D), lambda b: (b, 0, 0)),
        out_shape=jax.ShapeDtypeStruct((B, M, 2 * D), BF16),
        compiler_params=pltpu.CompilerParams(
            dimension_semantics=("arbitrary",), vmem_limit_bytes=VMEM_LIMIT),
        name="mem_kv",
    )(mem, g_mem, w_ckv)


def _diff_attn_kernel(q_ref, k_ref, vt_ref, ga_ref, gsub_ref, lq1_ref, lk1_ref,
                      lq2_ref, lk2_ref, o_ref, qs_ref, vx_ref, m_ref, acc_ref, st0_ref):
    qi = pl.program_id(2)
    tq = q_ref.shape[1]
    tk = ATTN_TK
    nkb = vx_ref.shape[0]

    @pl.when(qi == 0)
    def _():
        for j in range(nkb):
            vx_ref[j, 0:HEAD_WIDTH, :] = vt_ref[0, 0, :, j * tk:(j + 1) * tk]
            vx_ref[j, HEAD_WIDTH:, :] = jnp.ones((BF16_SUBLANES, tk), BF16)

    qt = q_ref[0].astype(F32).T
    feat = lax.broadcasted_iota(jnp.int32, qt.shape, 0)
    first = feat < DIFF_HEAD_DIM
    qs_ref[:, 0:tq] = jnp.where(first, qt, 0.0).astype(BF16)
    qs_ref[:, tq:] = jnp.where(first, 0.0, qt).astype(BF16)
    m_ref[...] = jnp.full(m_ref.shape, NEG_BIG, F32)
    acc_ref[...] = jnp.zeros(acc_ref.shape, F32)

    def scores(kb, c):
        start = pl.multiple_of(kb * tk, tk)
        return jnp.dot(k_ref[0, pl.ds(start, tk), :], qs_ref[:, c * tq:(c + 1) * tq],
                       preferred_element_type=F32)

    def softmax(st, c, masked):
        cols = slice(c * tq, (c + 1) * tq)
        if masked:
            key = lax.broadcasted_iota(jnp.int32, st.shape, 0)
            qry = lax.broadcasted_iota(jnp.int32, st.shape, 1)
            st = jnp.where(key <= qry, st, NEG_BIG)
        m_prev = m_ref[:, cols]
        m_new = jnp.maximum(m_prev, jnp.max(st, axis=0, keepdims=True))
        m_ref[:, cols] = m_new
        return jnp.exp2(m_prev - m_new), jnp.exp2(st - m_new).astype(BF16)

    def pv(kb, c, alpha, pt):
        cols = slice(c * tq, (c + 1) * tq)
        acc_ref[:, cols] = alpha * acc_ref[:, cols] + jnp.dot(
            vx_ref[kb], pt, preferred_element_type=F32)

    def block(kb, masked):
        st1 = scores(kb, 1)
        a0, p0 = softmax(st0_ref[...], 0, masked)
        pv(kb, 0, a0, p0)
        if not masked:
            st0_ref[...] = scores(kb + 1, 0)
        a1, p1 = softmax(st1, 1, masked)
        pv(kb, 1, a1, p1)

    st0_ref[...] = scores(0, 0)

    def body(u, carry):
        for j in range(ATTN_UNROLL):
            block(ATTN_UNROLL * u + j, False)
        return carry

    nfull = qi // ATTN_UNROLL
    lax.fori_loop(0, nfull, body, 0)

    for r in range(ATTN_UNROLL):
        @pl.when(qi % ATTN_UNROLL == r)
        def _(r=r):
            for j in range(r):
                block(nfull * ATTN_UNROLL + j, False)
            block(qi, True)

    lam = (jnp.exp(jnp.sum(lq1_ref[...] * lk1_ref[...], keepdims=True))
           - jnp.exp(jnp.sum(lq2_ref[...] * lk2_ref[...], keepdims=True)) + LAM_INIT)
    o1 = acc_ref[0:HEAD_WIDTH, 0:tq] / acc_ref[HEAD_WIDTH:HEAD_WIDTH + 1, 0:tq]
    o2 = acc_ref[0:HEAD_WIDTH, tq:] / acc_ref[HEAD_WIDTH:HEAD_WIDTH + 1, tq:]
    ot = o1 - lam * o2
    inv = lax.rsqrt(jnp.mean(ot * ot, axis=0, keepdims=True) + NORM_EPS)
    yt = ot * inv * gsub_ref[...] * (1.0 - LAM_INIT)
    o_ref[0] = (yt.T * ga_ref[0].astype(F32)).astype(BF16)


def _diff_attn(q, k, vt, ga, g_subln_col, lq1, lk1, lq2, lk2):
    B, S, _ = q.shape
    tq = ATTN_TQ
    assert tq == ATTN_TK
    qblk = pl.BlockSpec((1, tq, HEAD_WIDTH), lambda b, h, i: (b, i, h))
    kblk = pl.BlockSpec((1, S, HEAD_WIDTH), lambda b, h, i: (b, 0, h))
    vtblk = pl.BlockSpec((1, 1, HEAD_WIDTH, S), lambda b, h, i: (b, h, 0, 0))
    vec = lambda n: pl.BlockSpec((1, n), lambda b, h, i: (0, 0))
    return pl.pallas_call(
        _diff_attn_kernel,
        grid=(B, DIFF_HEADS, S // tq),
        in_specs=[qblk, kblk, vtblk, qblk,
                  pl.BlockSpec((HEAD_WIDTH, 1), lambda b, h, i: (0, 0)),
                  vec(DIFF_HEAD_DIM), vec(DIFF_HEAD_DIM), vec(DIFF_HEAD_DIM), vec(DIFF_HEAD_DIM)],
        out_specs=qblk,
        out_shape=jax.ShapeDtypeStruct((B, S, ATTN_WIDTH), BF16),
        scratch_shapes=[
            pltpu.VMEM((HEAD_WIDTH, 2 * tq), BF16),
            pltpu.VMEM((S // ATTN_TK, VT_ROWS, ATTN_TK), BF16),
            pltpu.VMEM((1, 2 * tq), F32),
            pltpu.VMEM((VT_ROWS, 2 * tq), F32),
            pltpu.VMEM((ATTN_TK, tq), F32),
        ],
        compiler_params=pltpu.CompilerParams(
            dimension_semantics=("arbitrary", "arbitrary", "arbitrary"),
            vmem_limit_bytes=VMEM_LIMIT),
        name="diff_attn",
    )(q, k, vt, ga, g_subln_col, lq1, lk1, lq2, lk2)


def _post_kernel(x_ref, mp_ref, ya_ref, wout_ref, gc_ref, wcq_ref, kv_ref, wco_ref,
                 gm_ref, wup_ref, wdn_ref, gf_ref, o_ref, oc_ref):
    x1 = (x_ref[0]
          + jnp.dot(mp_ref[0], wout_ref[0:POOL_WIDTH, :], preferred_element_type=F32)
          + jnp.dot(ya_ref[0], wout_ref[POOL_WIDTH:, :], preferred_element_type=F32))
    hc = _rms(x1, gc_ref[...]).astype(BF16)
    qc = (jnp.dot(hc, wcq_ref[...], preferred_element_type=F32)
          * CROSS_HEAD_DIM ** -0.5).astype(BF16)
    for h in range(CROSS_HEADS):
        sl = slice(h * CROSS_HEAD_DIM, (h + 1) * CROSS_HEAD_DIM)
        kh = kv_ref[0, :, sl]
        vh = kv_ref[0, :, D_MODEL + h * CROSS_HEAD_DIM:D_MODEL + (h + 1) * CROSS_HEAD_DIM]
        s = lax.dot_general(qc[:, sl], kh, (((1,), (1,)), ((), ())),
                            preferred_element_type=F32)
        p = jnp.exp(s - jnp.max(s, axis=1, keepdims=True))
        l = jnp.sum(p, axis=1, keepdims=True)
        oh = jnp.dot(p.astype(BF16), vh, preferred_element_type=F32) / l
        oc_ref[:, sl] = oh.astype(BF16)
    x2 = x1 + jnp.dot(oc_ref[...], wco_ref[...], preferred_element_type=F32)

    hm = _rms(x2, gm_ref[...]).astype(BF16)
    acc = x2
    for c in range(D_FF // FF_CHUNK):
        sl = slice(c * FF_CHUNK, (c + 1) * FF_CHUNK)
        up = jnp.dot(hm, wup_ref[:, sl], preferred_element_type=F32)
        a = jnp.square(jnp.maximum(up, 0.0)).astype(BF16)
        acc = acc + jnp.dot(a, wdn_ref[sl, :], preferred_element_type=F32)
    o_ref[0] = _rms(acc, gf_ref[...])


def _post(x, mp, ya, w_out, g_cross, w_cq, kv, w_co, g_mlp, w_up, w_down, g_final):
    B, S, D = x.shape
    ts = ROW_TILE
    M = kv.shape[1]
    row = lambda b, s: (b, s, 0)
    const2 = lambda b, s: (0, 0)
    resident = lambda shape: pl.BlockSpec(shape, const2, pipeline_mode=pl.Buffered(1))
    return pl.pallas_call(
        _post_kernel,
        grid=(B, S // ts),
        in_specs=[
            pl.BlockSpec((1, ts, D), row),
            pl.BlockSpec((1, ts, POOL_WIDTH), row),
            pl.BlockSpec((1, ts, ATTN_WIDTH), row),
            resident((D, D)),
            resident((1, D)),
            resident((D, D)),
            pl.BlockSpec((1, M, 2 * D), lambda b, s: (b, 0, 0)),
            resident((D, D)),
            resident((1, D)),
            resident((D, D_FF)),
            resident((D_FF, D)),
            resident((1, D)),
        ],
        out_specs=pl.BlockSpec((1, ts, D), row),
        out_shape=jax.ShapeDtypeStruct((B, S, D), F32),
        scratch_shapes=[pltpu.VMEM((ts, D), BF16)],
        compiler_params=pltpu.CompilerParams(
            dimension_semantics=("arbitrary", "arbitrary"),
            vmem_limit_bytes=VMEM_LIMIT),
        name="post",
    )(x, mp, ya, w_out, g_cross, w_cq, kv, w_co, g_mlp, w_up, w_down, g_final)


def _rope_tables(S):
    dh = DIFF_HEAD_DIM
    inv_freq = ROPE_THETA ** (-jnp.arange(0, dh, 2, dtype=F32) / dh)
    inv_freq = jnp.tile(inv_freq, LANES // (dh // 2))
    sign = jnp.where((jnp.arange(LANES) % dh) < dh // 2, -1.0, 1.0).astype(F32)
    assert S % LANES == 0
    ang_hi = (jnp.arange(S // LANES, dtype=F32) * LANES)[:, None] * inv_freq[None, :]
    ang_lo = jnp.arange(LANES, dtype=F32)[:, None] * inv_freq[None, :]
    ch, sh, cl, sl = lax.optimization_barrier(
        (jnp.cos(ang_hi), jnp.sin(ang_hi), jnp.cos(ang_lo), jnp.sin(ang_lo)))
    ch, sh, cl, sl = ch[:, None, :], sh[:, None, :], cl[None, :, :], sl[None, :, :]
    cos_t = (ch * cl - sh * sl).reshape(S, LANES)
    sin_t = ((sh * cl + ch * sl) * sign).reshape(S, LANES)
    return cos_t, sin_t


def kernel(x, mem, g_mix, w_in, w_pool, pool_scale, lambda_q1, lambda_k1, lambda_q2,
           lambda_k2, g_subln, w_out, g_cross, g_mem, w_cq, w_ckv, w_co, g_mlp, w_up,
           w_down, g_final):
    B, S, _ = x.shape
    assert g_mix.shape[0] == 1, "single-layer block"
    cos_t, sin_t = _rope_tables(S)
    mp, q, k, vt, ga = _in_proj(x, g_mix, w_in[0].astype(BF16), w_pool[0].astype(BF16),
                                pool_scale, cos_t, sin_t)
    kv = _mem_kv(mem, g_mem, w_ckv[0].astype(BF16))
    ya = _diff_attn(q, k, vt, ga, g_subln.reshape(HEAD_WIDTH, 1), lambda_q1, lambda_k1,
                    lambda_q2, lambda_k2)
    return _post(x, mp, ya, w_out[0].astype(BF16), g_cross, w_cq[0].astype(BF16), kv,
                 w_co[0].astype(BF16), g_mlp, w_up[0].astype(BF16), w_down[0].astype(BF16),
                 g_final.reshape(1, D_MODEL))
```

```python
import math

import jax
import jax.numpy as jnp
from jax import lax
from jax.experimental import pallas as pl
from jax.experimental.pallas import tpu as pltpu

D_MODEL = 1024
POOL_WIDTH = 512
POOL_WINDOWS = (2, 4, 8, 16)
POOL_GROUP_WIDTH = 128
POOL_HALO = 16
DIFF_HEADS = 4
DIFF_HEAD_DIM = 64
HEAD_WIDTH = 2 * DIFF_HEAD_DIM
Q_WIDTH = 512
ATTN_WIDTH = 512
ROPE_THETA = 10000.0
CROSS_HEADS = 4
CROSS_HEAD_DIM = 256
D_FF = 4096
NORM_EPS = 1e-6
NEG_BIG = -1e30
LAM_INIT = 0.8 - 0.6 * math.exp(-0.3 * 0)
LOG2E = math.log2(math.e)

LANES = 128
BF16_SUBLANES = 16
VMEM_LIMIT = 60 * 1024 * 1024

PROJ_TILE = 1024
ATTN_TQ = 512
ATTN_TK = 512
ATTN_UNROLL = 4
ROW_TILE = 1024
FF_CHUNK = 1024
VT_ROWS = HEAD_WIDTH + BF16_SUBLANES

BF16 = jnp.bfloat16
F32 = jnp.float32


def _rms(x, g):
    return x * lax.rsqrt(jnp.mean(x * x, axis=-1, keepdims=True) + NORM_EPS) * g


def _rotate_half_pairs(x):
    lane = lax.broadcasted_iota(jnp.int32, x.shape, 1)
    lower = (lane % DIFF_HEAD_DIM) < (DIFF_HEAD_DIM // 2)
    return jnp.where(lower, pltpu.roll(x, LANES - 32, axis=1), pltpu.roll(x, 32, axis=1))


def _in_proj_kernel(x_ref, g_ref, w_ref, wpool_ref, pscale_ref, cos_ref, sin_ref,
                    mp_ref, q_ref, k_ref, vt_ref, ga_ref, ext_ref):
    si = pl.program_id(1)
    ts = x_ref.shape[1]
    h = _rms(x_ref[0], g_ref[...]).astype(BF16)

    def proj(lo, hi):
        return jnp.dot(h, w_ref[:, lo:hi], preferred_element_type=F32)

    gates = 1.0 / (1.0 + jnp.exp(-proj(2048, 3072)))
    ga_ref[0] = gates[:, POOL_WIDTH:].astype(BF16)

    @pl.when(si == 0)
    def _():
        ext_ref[0:POOL_HALO, :] = jnp.zeros((POOL_HALO, POOL_WIDTH), F32)

    ext_ref[POOL_HALO:, :] = proj(0, POOL_WIDTH)
    q = proj(512, 1024)
    k = proj(1024, 1536)
    v = proj(1536, 2048)

    pos = si * ts + lax.broadcasted_iota(jnp.int32, (ts, 1), 0)
    zs = []
    for g, w in enumerate(POOL_WINDOWS):
        lo, hi = g * POOL_GROUP_WIDTH, (g + 1) * POOL_GROUP_WIDTH
        e = ext_ref[:, lo:hi]
        win = e
        shift = 1
        while shift < w:
            win = win + pltpu.roll(win, shift, axis=0)
            shift *= 2
        cnt = jnp.minimum(pos + 1, w).astype(F32)
        zs.append((win[POOL_HALO:] / cnt - e[POOL_HALO:]).astype(BF16))
    ext_ref[0:POOL_HALO, :] = ext_ref[ts:ts + POOL_HALO, :]

    cos = cos_ref[...]
    sin = sin_ref[...]
    qscale = DIFF_HEAD_DIM ** -0.5 * LOG2E
    for c in range(Q_WIDTH // LANES):
        sl = slice(c * LANES, (c + 1) * LANES)
        qc = q[:, sl]
        kc = k[:, sl]
        q_ref[0, :, sl] = ((qc * cos + _rotate_half_pairs(qc) * sin) * qscale).astype(BF16)
        k_ref[0, :, sl] = (kc * cos + _rotate_half_pairs(kc) * sin).astype(BF16)

    for hd in range(DIFF_HEADS):
        vt_ref[0, hd] = v[:, hd * HEAD_WIDTH:(hd + 1) * HEAD_WIDTH].T.astype(BF16)

    for g in range(len(POOL_WINDOWS)):
        lo, hi = g * POOL_GROUP_WIDTH, (g + 1) * POOL_GROUP_WIDTH
        y = jnp.dot(zs[g], wpool_ref[g], preferred_element_type=F32)
        y = y * pscale_ref[:, lo:hi] * gates[:, lo:hi]
        mp_ref[0, :, lo:hi] = y.astype(BF16)


def _in_proj(x, g_mix, w_in, w_pool, pool_scale, cos_t, sin_t):
    B, S, D = x.shape
    ts = PROJ_TILE
    row = lambda b, s: (b, s, 0)
    const2 = lambda b, s: (0, 0)
    half = pl.BlockSpec((1, ts, 512), row)
    half_sds = jax.ShapeDtypeStruct((B, S, 512), BF16)
    vt_spec = pl.BlockSpec((1, DIFF_HEADS, HEAD_WIDTH, ts), lambda b, s: (b, 0, 0, s))
    vt_sds = jax.ShapeDtypeStruct((B, DIFF_HEADS, HEAD_WIDTH, S), BF16)
    return pl.pallas_call(
        _in_proj_kernel,
        grid=(B, S // ts),
        in_specs=[
            pl.BlockSpec((1, ts, D), row),
            pl.BlockSpec((1, D), const2),
            pl.BlockSpec((D, 3072), const2),
            pl.BlockSpec((4, 128, 128), lambda b, s: (0, 0, 0)),
            pl.BlockSpec((1, POOL_WIDTH), const2),
            pl.BlockSpec((ts, LANES), lambda b, s: (s, 0)),
            pl.BlockSpec((ts, LANES), lambda b, s: (s, 0)),
        ],
        out_specs=[half, half, half, vt_spec, half],
        out_shape=[half_sds, half_sds, half_sds, vt_sds, half_sds],
        scratch_shapes=[pltpu.VMEM((ts + POOL_HALO, POOL_WIDTH), F32)],
        compiler_params=pltpu.CompilerParams(
            dimension_semantics=("arbitrary", "arbitrary"),
            vmem_limit_bytes=VMEM_LIMIT),
        name="in_proj",
    )(x, g_mix, w_in, w_pool, pool_scale, cos_t, sin_t)


def _mem_kv_kernel(mem_ref, g_ref, w_ref, kv_ref):
    mn = _rms(mem_ref[0], g_ref[...]).astype(BF16)
    kv_ref[0] = jnp.dot(mn, w_ref[...], preferred_element_type=F32).astype(BF16)


def _mem_kv(mem, g_mem, w_ckv):
    B, M, D = mem.shape
    return pl.pallas_call(
        _mem_kv_kernel,
        grid=(B,),
        in_specs=[
            pl.BlockSpec((1, M, D), lambda b: (b, 0, 0)),
            pl.BlockSpec((1, D), lambda b: (0, 0)),
            pl.BlockSpec((D, 2 * D), lambda b: (0, 0)),
        ],
        out_specs=pl.BlockSpec((1, M, 2 * D), lambda b: (b, 0, 0)),
        out_shape=jax.ShapeDtypeStruct((B, M, 2 * D), BF16),
        compiler_params=pltpu.CompilerParams(
            dimension_semantics=("arbitrary",), vmem_limit_bytes=VMEM_LIMIT),
        name="mem_kv",
    )(mem, g_mem, w_ckv)


def _diff_attn_kernel(q_ref, k_ref, vt_ref, ga_ref, gsub_ref, lq1_ref, lk1_ref,
                      lq2_ref, lk2_ref, o_ref, qs_ref, vx_ref, m_ref, acc_ref, st0_ref, st1_ref):
    qi = pl.program_id(2)
    tq = q_ref.shape[1]
    tk = ATTN_TK
    nkb = vx_ref.shape[0]

    @pl.when(qi == 0)
    def _():
        for j in range(nkb):
            vx_ref[j, 0:HEAD_WIDTH, :] = vt_ref[0, 0, :, j * tk:(j + 1) * tk]
            vx_ref[j, HEAD_WIDTH:, :] = jnp.ones((BF16_SUBLANES, tk), BF16)

    qt = q_ref[0].astype(F32).T
    feat = lax.broadcasted_iota(jnp.int32, qt.shape, 0)
    first = feat < DIFF_HEAD_DIM
    qs_ref[:, 0:tq] = jnp.where(first, qt, 0.0).astype(BF16)
    qs_ref[:, tq:] = jnp.where(first, 0.0, qt).astype(BF16)
    m_ref[...] = jnp.full(m_ref.shape, NEG_BIG, F32)
    acc_ref[...] = jnp.zeros(acc_ref.shape, F32)

    def scores(kb, c):
        start = pl.multiple_of(kb * tk, tk)
        return jnp.dot(k_ref[0, pl.ds(start, tk), :], qs_ref[:, c * tq:(c + 1) * tq],
                       preferred_element_type=F32)

    def softmax(st, c, masked):
        cols = slice(c * tq, (c + 1) * tq)
        if masked:
            key = lax.broadcasted_iota(jnp.int32, st.shape, 0)
            qry = lax.broadcasted_iota(jnp.int32, st.shape, 1)
            st = jnp.where(key <= qry, st, NEG_BIG)
        m_prev = m_ref[:, cols]
        m_new = jnp.maximum(m_prev, jnp.max(st, axis=0, keepdims=True))
        m_ref[:, cols] = m_new
        return jnp.exp2(m_prev - m_new), jnp.exp2(st - m_new).astype(BF16)

    def pv(kb, c, alpha, pt):
        cols = slice(c * tq, (c + 1) * tq)
        acc_ref[:, cols] = alpha * acc_ref[:, cols] + jnp.dot(
            vx_ref[kb], pt, preferred_element_type=F32)

    def block(kb, masked):
        st1_ref[...] = scores(kb, 1)
        a0, p0 = softmax(st0_ref[...], 0, masked)
        pv(kb, 0, a0, p0)
        if not masked:
            st0_ref[...] = scores(kb + 1, 0)
        a1, p1 = softmax(st1_ref[...], 1, masked)
        pv(kb, 1, a1, p1)

    st0_ref[...] = scores(0, 0)

    def body(u, carry):
        for j in range(ATTN_UNROLL):
            block(ATTN_UNROLL * u + j, False)
        return carry

    nfull = qi // ATTN_UNROLL
    lax.fori_loop(0, nfull, body, 0)

    for r in range(ATTN_UNROLL):
        @pl.when(qi % ATTN_UNROLL == r)
        def _(r=r):
            for j in range(r):
                block(nfull * ATTN_UNROLL + j, False)
            block(qi, True)

    lam = (jnp.exp(jnp.sum(lq1_ref[...] * lk1_ref[...], keepdims=True))
           - jnp.exp(jnp.sum(lq2_ref[...] * lk2_ref[...], keepdims=True)) + LAM_INIT)
    o1 = acc_ref[0:HEAD_WIDTH, 0:tq] / acc_ref[HEAD_WIDTH:HEAD_WIDTH + 1, 0:tq]
    o2 = acc_ref[0:HEAD_WIDTH, tq:] / acc_ref[HEAD_WIDTH:HEAD_WIDTH + 1, tq:]
    ot = o1 - lam * o2
    inv = lax.rsqrt(jnp.mean(ot * ot, axis=0, keepdims=True) + NORM_EPS)
    yt = ot * inv * gsub_ref[...] * (1.0 - LAM_INIT)
    o_ref[0] = (yt.T * ga_ref[0].astype(F32)).astype(BF16)


def _diff_attn(q, k, vt, ga, g_subln_col, lq1, lk1, lq2, lk2):
    B, S, _ = q.shape
    tq = ATTN_TQ
    assert tq == ATTN_TK
    qblk = pl.BlockSpec((1, tq, HEAD_WIDTH), lambda b, h, i: (b, i, h))
    kblk = pl.BlockSpec((1, S, HEAD_WIDTH), lambda b, h, i: (b, 0, h))
    vtblk = pl.BlockSpec((1, 1, HEAD_WIDTH, S), lambda b, h, i: (b, h, 0, 0))
    vec = lambda n: pl.BlockSpec((1, n), lambda b, h, i: (0, 0))
    return pl.pallas_call(
        _diff_attn_kernel,
        grid=(B, DIFF_HEADS, S // tq),
        in_specs=[qblk, kblk, vtblk, qblk,
                  pl.BlockSpec((HEAD_WIDTH, 1), lambda b, h, i: (0, 0)),
                  vec(DIFF_HEAD_DIM), vec(DIFF_HEAD_DIM), vec(DIFF_HEAD_DIM), vec(DIFF_HEAD_DIM)],
        out_specs=qblk,
        out_shape=jax.ShapeDtypeStruct((B, S, ATTN_WIDTH), BF16),
        scratch_shapes=[
            pltpu.VMEM((HEAD_WIDTH, 2 * tq), BF16),
            pltpu.VMEM((S // ATTN_TK, VT_ROWS, ATTN_TK), BF16),
            pltpu.VMEM((1, 2 * tq), F32),
            pltpu.VMEM((VT_ROWS, 2 * tq), F32),
            pltpu.VMEM((ATTN_TK, tq), F32),
            pltpu.VMEM((ATTN_TK, tq), F32),
        ],
        compiler_params=pltpu.CompilerParams(
            dimension_semantics=("arbitrary", "arbitrary", "arbitrary"),
            vmem_limit_bytes=VMEM_LIMIT),
        name="diff_attn",
    )(q, k, vt, ga, g_subln_col, lq1, lk1, lq2, lk2)


def _post_kernel(x_ref, mp_ref, ya_ref, wout_ref, gc_ref, wcq_ref, kv_ref, wco_ref,
                 gm_ref, wup_ref, wdn_ref, gf_ref, o_ref, oc_ref):
    x1 = (x_ref[0]
          + jnp.dot(mp_ref[0], wout_ref[0:POOL_WIDTH, :], preferred_element_type=F32)
          + jnp.dot(ya_ref[0], wout_ref[POOL_WIDTH:, :], preferred_element_type=F32))
    hc = _rms(x1, gc_ref[...]).astype(BF16)
    qc = (jnp.dot(hc, wcq_ref[...], preferred_element_type=F32)
          * CROSS_HEAD_DIM ** -0.5).astype(BF16)
    for h in range(CROSS_HEADS):
        sl = slice(h * CROSS_HEAD_DIM, (h + 1) * CROSS_HEAD_DIM)
        kh = kv_ref[0, :, sl]
        vh = kv_ref[0, :, D_MODEL + h * CROSS_HEAD_DIM:D_MODEL + (h + 1) * CROSS_HEAD_DIM]
        s = lax.dot_general(qc[:, sl], kh, (((1,), (1,)), ((), ())),
                            preferred_element_type=F32)
        p = jnp.exp(s - jnp.max(s, axis=1, keepdims=True))
        l = jnp.sum(p, axis=1, keepdims=True)
        oh = jnp.dot(p.astype(BF16), vh, preferred_element_type=F32) / l
        oc_ref[:, sl] = oh.astype(BF16)
    x2 = x1 + jnp.dot(oc_ref[...], wco_ref[...], preferred_element_type=F32)

    hm = _rms(x2, gm_ref[...]).astype(BF16)
    acc = x2
    for c in range(D_FF // FF_CHUNK):
        sl = slice(c * FF_CHUNK, (c + 1) * FF_CHUNK)
        up = jnp.dot(hm, wup_ref[:, sl], preferred_element_type=F32)
        a = jnp.square(jnp.maximum(up, 0.0)).astype(BF16)
        acc = acc + jnp.dot(a, wdn_ref[sl, :], preferred_element_type=F32)
    o_ref[0] = _rms(acc, gf_ref[...])


def _post(x, mp, ya, w_out, g_cross, w_cq, kv, w_co, g_mlp, w_up, w_down, g_final):
    B, S, D = x.shape
    ts = ROW_TILE
    M = kv.shape[1]
    row = lambda b, s: (b, s, 0)
    const2 = lambda b, s: (0, 0)
    resident = lambda shape: pl.BlockSpec(shape, const2, pipeline_mode=pl.Buffered(1))
    return pl.pallas_call(
        _post_kernel,
        grid=(B, S // ts),
        in_specs=[
            pl.BlockSpec((1, ts, D), row),
            pl.BlockSpec((1, ts, POOL_WIDTH), row),
            pl.BlockSpec((1, ts, ATTN_WIDTH), row),
            resident((D, D)),
            resident((1, D)),
            resident((D, D)),
            pl.BlockSpec((1, M, 2 * D), lambda b, s: (b, 0, 0)),
            resident((D, D)),
            resident((1, D)),
            resident((D, D_FF)),
            resident((D_FF, D)),
            resident((1, D)),
        ],
        out_specs=pl.BlockSpec((1, ts, D), row),
        out_shape=jax.ShapeDtypeStruct((B, S, D), F32),
        scratch_shapes=[pltpu.VMEM((ts, D), BF16)],
        compiler_params=pltpu.CompilerParams(
            dimension_semantics=("arbitrary", "arbitrary"),
            vmem_limit_bytes=VMEM_LIMIT),
        name="post",
    )(x, mp, ya, w_out, g_cross, w_cq, kv, w_co, g_mlp, w_up, w_down, g_final)


def _rope_tables(S):
    dh = DIFF_HEAD_DIM
    inv_freq = ROPE_THETA ** (-jnp.arange(0, dh, 2, dtype=F32) / dh)
    inv_freq = jnp.tile(inv_freq, LANES // (dh // 2))
    sign = jnp.where((jnp.arange(LANES) % dh) < dh // 2, -1.0, 1.0).astype(F32)
    assert S % LANES == 0
    ang_hi = (jnp.arange(S // LANES, dtype=F32) * LANES)[:, None] * inv_freq[None, :]
    ang_lo = jnp.arange(LANES, dtype=F32)[:, None] * inv_freq[None, :]
    ch, sh, cl, sl = lax.optimization_barrier(
        (jnp.cos(ang_hi), jnp.sin(ang_hi), jnp.cos(ang_lo), jnp.sin(ang_lo)))
    ch, sh, cl, sl = ch[:, None, :], sh[:, None, :], cl[None, :, :], sl[None, :, :]
    cos_t = (ch * cl - sh * sl).reshape(S, LANES)
    sin_t = ((sh * cl + ch * sl) * sign).reshape(S, LANES)
    return cos_t, sin_t


def kernel(x, mem, g_mix, w_in, w_pool, pool_scale, lambda_q1, lambda_k1, lambda_q2,
           lambda_k2, g_subln, w_out, g_cross, g_mem, w_cq, w_ckv, w_co, g_mlp, w_up,
           w_down, g_final):
    B, S, _ = x.shape
    assert g_mix.shape[0] == 1, "single-layer block"
    cos_t, sin_t = _rope_tables(S)
    mp, q, k, vt, ga = _in_proj(x, g_mix, w_in[0].astype(BF16), w_pool[0].astype(BF16),
                                pool_scale, cos_t, sin_t)
    kv = _mem_kv(mem, g_mem, w_ckv[0].astype(BF16))
    ya = _diff_attn(q, k, vt, ga, g_subln.reshape(HEAD_WIDTH, 1), lambda_q1, lambda_k1,
                    lambda_q2, lambda_k2)
    return _post(x, mp, ya, w_out[0].astype(BF16), g_cross, w_cq[0].astype(BF16), kv,
                 w_co[0].astype(BF16), g_mlp, w_up[0].astype(BF16), w_down[0].astype(BF16),
                 g_final.reshape(1, D_MODEL))
```

```python
import math

import jax
import jax.numpy as jnp
from jax import lax
from jax.experimental import pallas as pl
from jax.experimental.pallas import tpu as pltpu

D_MODEL = 1024
POOL_WIDTH = 512
POOL_WINDOWS = (2, 4, 8, 16)
POOL_GROUP_WIDTH = 128
POOL_HALO = 16
DIFF_HEADS = 4
DIFF_HEAD_DIM = 64
HEAD_WIDTH = 2 * DIFF_HEAD_DIM
Q_WIDTH = 512
ATTN_WIDTH = 512
ROPE_THETA = 10000.0
CROSS_HEADS = 4
CROSS_HEAD_DIM = 256
D_FF = 4096
NORM_EPS = 1e-6
NEG_BIG = -1e30
LAM_INIT = 0.8 - 0.6 * math.exp(-0.3 * 0)
LOG2E = math.log2(math.e)

LANES = 128
BF16_SUBLANES = 16
VMEM_LIMIT = 60 * 1024 * 1024

PROJ_TILE = 1024
ATTN_TQ = 512
ATTN_TK = 512
ATTN_UNROLL = 8
ROW_TILE = 1024
FF_CHUNK = 1024
VT_ROWS = HEAD_WIDTH + BF16_SUBLANES

BF16 = jnp.bfloat16
F32 = jnp.float32


def _rms(x, g):
    return x * lax.rsqrt(jnp.mean(x * x, axis=-1, keepdims=True) + NORM_EPS) * g


def _rotate_half_pairs(x):
    lane = lax.broadcasted_iota(jnp.int32, x.shape, 1)
    lower = (lane % DIFF_HEAD_DIM) < (DIFF_HEAD_DIM // 2)
    return jnp.where(lower, pltpu.roll(x, LANES - 32, axis=1), pltpu.roll(x, 32, axis=1))


def _in_proj_kernel(x_ref, g_ref, w_ref, wpool_ref, pscale_ref, cos_ref, sin_ref,
                    mp_ref, q_ref, k_ref, vt_ref, ga_ref, ext_ref):
    si = pl.program_id(1)
    ts = x_ref.shape[1]
    h = _rms(x_ref[0], g_ref[...]).astype(BF16)

    def proj(lo, hi):
        return jnp.dot(h, w_ref[:, lo:hi], preferred_element_type=F32)

    gates = 1.0 / (1.0 + jnp.exp(-proj(2048, 3072)))
    for hd in range(DIFF_HEADS):
        lo = POOL_WIDTH + hd * HEAD_WIDTH
        ga_ref[0, hd] = gates[:, lo:lo + HEAD_WIDTH].astype(BF16)

    @pl.when(si == 0)
    def _():
        ext_ref[0:POOL_HALO, :] = jnp.zeros((POOL_HALO, POOL_WIDTH), F32)

    ext_ref[POOL_HALO:, :] = proj(0, POOL_WIDTH)
    q = proj(512, 1024)
    k = proj(1024, 1536)
    v = proj(1536, 2048)

    pos = si * ts + lax.broadcasted_iota(jnp.int32, (ts, 1), 0)
    zs = []
    for g, w in enumerate(POOL_WINDOWS):
        lo, hi = g * POOL_GROUP_WIDTH, (g + 1) * POOL_GROUP_WIDTH
        e = ext_ref[:, lo:hi]
        win = e
        shift = 1
        while shift < w:
            win = win + pltpu.roll(win, shift, axis=0)
            shift *= 2
        cnt = jnp.minimum(pos + 1, w).astype(F32)
        zs.append((win[POOL_HALO:] / cnt - e[POOL_HALO:]).astype(BF16))
    ext_ref[0:POOL_HALO, :] = ext_ref[ts:ts + POOL_HALO, :]

    cos = cos_ref[...]
    sin = sin_ref[...]
    qscale = DIFF_HEAD_DIM ** -0.5 * LOG2E
    for c in range(Q_WIDTH // LANES):
        sl = slice(c * LANES, (c + 1) * LANES)
        qc = q[:, sl]
        kc = k[:, sl]
        q_ref[0, c] = ((qc * cos + _rotate_half_pairs(qc) * sin) * qscale).astype(BF16)
        k_ref[0, c] = (kc * cos + _rotate_half_pairs(kc) * sin).astype(BF16)

    for hd in range(DIFF_HEADS):
        vt_ref[0, hd] = v[:, hd * HEAD_WIDTH:(hd + 1) * HEAD_WIDTH].T.astype(BF16)

    for g in range(len(POOL_WINDOWS)):
        lo, hi = g * POOL_GROUP_WIDTH, (g + 1) * POOL_GROUP_WIDTH
        y = jnp.dot(zs[g], wpool_ref[g], preferred_element_type=F32)
        y = y * pscale_ref[:, lo:hi] * gates[:, lo:hi]
        mp_ref[0, :, lo:hi] = y.astype(BF16)


def _in_proj(x, g_mix, w_in, w_pool, pool_scale, cos_t, sin_t):
    B, S, D = x.shape
    ts = PROJ_TILE
    row = lambda b, s: (b, s, 0)
    const2 = lambda b, s: (0, 0)
    half = pl.BlockSpec((1, ts, 512), row)
    half_sds = jax.ShapeDtypeStruct((B, S, 512), BF16)
    heads = pl.BlockSpec((1, DIFF_HEADS, ts, HEAD_WIDTH), lambda b, s: (b, 0, s, 0))
    heads_sds = jax.ShapeDtypeStruct((B, DIFF_HEADS, S, HEAD_WIDTH), BF16)
    vt_spec = pl.BlockSpec((1, DIFF_HEADS, HEAD_WIDTH, ts), lambda b, s: (b, 0, 0, s))
    vt_sds = jax.ShapeDtypeStruct((B, DIFF_HEADS, HEAD_WIDTH, S), BF16)
    return pl.pallas_call(
        _in_proj_kernel,
        grid=(B, S // ts),
        in_specs=[
            pl.BlockSpec((1, ts, D), row),
            pl.BlockSpec((1, D), const2),
            pl.BlockSpec((D, 3072), const2),
            pl.BlockSpec((4, 128, 128), lambda b, s: (0, 0, 0)),
            pl.BlockSpec((1, POOL_WIDTH), const2),
            pl.BlockSpec((ts, LANES), lambda b, s: (s, 0)),
            pl.BlockSpec((ts, LANES), lambda b, s: (s, 0)),
        ],
        out_specs=[half, heads, heads, vt_spec, heads],
        out_shape=[half_sds, heads_sds, heads_sds, vt_sds, heads_sds],
        scratch_shapes=[pltpu.VMEM((ts + POOL_HALO, POOL_WIDTH), F32)],
        compiler_params=pltpu.CompilerParams(
            dimension_semantics=("arbitrary", "arbitrary"),
            vmem_limit_bytes=VMEM_LIMIT),
        name="in_proj",
    )(x, g_mix, w_in, w_pool, pool_scale, cos_t, sin_t)


def _mem_kv_kernel(mem_ref, g_ref, w_ref, kv_ref):
    mn = _rms(mem_ref[0], g_ref[...]).astype(BF16)
    kv_ref[0] = jnp.dot(mn, w_ref[...], preferred_element_type=F32).astype(BF16)


def _mem_kv(mem, g_mem, w_ckv):
    B, M, D = mem.shape
    return pl.pallas_call(
        _mem_kv_kernel,
        grid=(B,),
        in_specs=[
            pl.BlockSpec((1, M, D), lambda b: (b, 0, 0)),
            pl.BlockSpec((1, D), lambda b: (0, 0)),
            pl.BlockSpec((D, 2 * D), lambda b: (0, 0)),
        ],
        out_specs=pl.BlockSpec((1, M, 2 * D), lambda b: (b, 0, 0)),
        out_shape=jax.ShapeDtypeStruct((B, M, 2 * D), BF16),
        compiler_params=pltpu.CompilerParams(
            dimension_semantics=("arbitrary",), vmem_limit_bytes=VMEM_LIMIT),
        name="mem_kv",
    )(mem, g_mem, w_ckv)


def _diff_attn_kernel(q_ref, k_ref, vt_ref, ga_ref, gsub_ref, lq1_ref, lk1_ref,
                      lq2_ref, lk2_ref, o_ref, qs_ref, vx_ref, m_ref, acc_ref, st0_ref, st1_ref):
    tq = ATTN_TQ
    tk = ATTN_TK
    nkb = vx_ref.shape[0]
    nq = q_ref.shape[2] // tq

    for j in range(nkb):
        vx_ref[j, 0:HEAD_WIDTH, :] = vt_ref[0, 0, :, j * tk:(j + 1) * tk]
        vx_ref[j, HEAD_WIDTH:, :] = jnp.ones((BF16_SUBLANES, tk), BF16)

    lam = (jnp.exp(jnp.sum(lq1_ref[...] * lk1_ref[...], keepdims=True))
           - jnp.exp(jnp.sum(lq2_ref[...] * lk2_ref[...], keepdims=True)) + LAM_INIT)

    def scores(kb, c):
        start = pl.multiple_of(kb * tk, tk)
        return jnp.dot(k_ref[0, 0, pl.ds(start, tk), :], qs_ref[:, c * tq:(c + 1) * tq],
                       preferred_element_type=F32)

    def softmax(st, c, masked):
        cols = slice(c * tq, (c + 1) * tq)
        if masked:
            key = lax.broadcasted_iota(jnp.int32, st.shape, 0)
            qry = lax.broadcasted_iota(jnp.int32, st.shape, 1)
            st = jnp.where(key <= qry, st, NEG_BIG)
        m_prev = m_ref[:, cols]
        m_new = jnp.maximum(m_prev, jnp.max(st, axis=0, keepdims=True))
        m_ref[:, cols] = m_new
        return jnp.exp2(m_prev - m_new), jnp.exp2(st - m_new).astype(BF16)

    def pv(kb, c, alpha, pt):
        cols = slice(c * tq, (c + 1) * tq)
        acc_ref[:, cols] = alpha * acc_ref[:, cols] + jnp.dot(
            vx_ref[kb], pt, preferred_element_type=F32)

    def block(kb, masked):
        st1_ref[...] = scores(kb, 1)
        a0, p0 = softmax(st0_ref[...], 0, masked)
        pv(kb, 0, a0, p0)
        if not masked:
            st0_ref[...] = scores(kb + 1, 0)
        a1, p1 = softmax(st1_ref[...], 1, masked)
        pv(kb, 1, a1, p1)

    def group(base, n):
        for j in range(n):
            block(base + j, False)

    def rows_of(qi):
        return pl.ds(pl.multiple_of(qi * tq, tq), tq)

    def start_tile(qi):
        qt = q_ref[0, 0, rows_of(qi), :].astype(F32).T
        first = lax.broadcasted_iota(jnp.int32, qt.shape, 0) < DIFF_HEAD_DIM
        qs_ref[:, 0:tq] = jnp.where(first, qt, 0.0).astype(BF16)
        qs_ref[:, tq:] = jnp.where(first, 0.0, qt).astype(BF16)
        m_ref[...] = jnp.full(m_ref.shape, NEG_BIG, F32)
        acc_ref[...] = jnp.zeros(acc_ref.shape, F32)
        st0_ref[...] = scores(0, 0)

    def key_blocks(qi):
        def body(u, carry):
            group(ATTN_UNROLL * u, ATTN_UNROLL)
            return carry

        lax.fori_loop(0, qi // ATTN_UNROLL, body, 0)
        half = ATTN_UNROLL // 2
        done = (qi // ATTN_UNROLL) * ATTN_UNROLL
        take_half = (qi - done) >= half

        @pl.when(take_half)
        def _():
            group(done, half)

        done = done + jnp.where(take_half, half, 0)
        for r in range(half):
            @pl.when(qi - done == r)
            def _(r=r):
                group(done, r)
                block(qi, True)

    def finish_tile(qi):
        o1 = acc_ref[0:HEAD_WIDTH, 0:tq] / acc_ref[HEAD_WIDTH:HEAD_WIDTH + 1, 0:tq]
        o2 = acc_ref[0:HEAD_WIDTH, tq:] / acc_ref[HEAD_WIDTH:HEAD_WIDTH + 1, tq:]
        ot = o1 - lam * o2
        inv = lax.rsqrt(jnp.mean(ot * ot, axis=0, keepdims=True) + NORM_EPS)
        yt = ot * inv * gsub_ref[...] * (1.0 - LAM_INIT)
        rows = rows_of(qi)
        o_ref[0, 0, rows, :] = (yt.T * ga_ref[0, 0, rows, :].astype(F32)).astype(BF16)

    start_tile(0)
    block(0, True)

    def tile(qi, carry):
        finish_tile(qi - 1)
        start_tile(qi)
        key_blocks(qi)
        return carry

    lax.fori_loop(1, nq, tile, 0)
    finish_tile(nq - 1)


def _diff_attn(q, k, vt, ga, g_subln_col, lq1, lk1, lq2, lk2):
    B, _, S, _ = q.shape
    tq = ATTN_TQ
    assert tq == ATTN_TK and S % tq == 0
    seq = pl.BlockSpec((1, 1, S, HEAD_WIDTH), lambda b, h: (b, h, 0, 0))
    vtblk = pl.BlockSpec((1, 1, HEAD_WIDTH, S), lambda b, h: (b, h, 0, 0))
    vec = lambda n: pl.BlockSpec((1, n), lambda b, h: (0, 0))
    return pl.pallas_call(
        _diff_attn_kernel,
        grid=(B, DIFF_HEADS),
        in_specs=[seq, seq, vtblk, seq,
                  pl.BlockSpec((HEAD_WIDTH, 1), lambda b, h: (0, 0)),
                  vec(DIFF_HEAD_DIM), vec(DIFF_HEAD_DIM), vec(DIFF_HEAD_DIM), vec(DIFF_HEAD_DIM)],
        out_specs=seq,
        out_shape=jax.ShapeDtypeStruct((B, DIFF_HEADS, S, HEAD_WIDTH), BF16),
        scratch_shapes=[
            pltpu.VMEM((HEAD_WIDTH, 2 * tq), BF16),
            pltpu.VMEM((S // ATTN_TK, VT_ROWS, ATTN_TK), BF16),
            pltpu.VMEM((1, 2 * tq), F32),
            pltpu.VMEM((VT_ROWS, 2 * tq), F32),
            pltpu.VMEM((ATTN_TK, tq), F32),
            pltpu.VMEM((ATTN_TK, tq), F32),
        ],
        compiler_params=pltpu.CompilerParams(
            dimension_semantics=("arbitrary", "arbitrary"),
            vmem_limit_bytes=VMEM_LIMIT),
        name="diff_attn",
    )(q, k, vt, ga, g_subln_col, lq1, lk1, lq2, lk2)


def _post_kernel(x_ref, mp_ref, ya_ref, wout_ref, gc_ref, wcq_ref, kv_ref, wco_ref,
                 gm_ref, wup_ref, wdn_ref, gf_ref, o_ref, oc_ref):
    x1 = (x_ref[0]
          + jnp.dot(mp_ref[0], wout_ref[0:POOL_WIDTH, :], preferred_element_type=F32)
          + jnp.dot(jnp.concatenate([ya_ref[0, hd] for hd in range(DIFF_HEADS)], axis=1),
                    wout_ref[POOL_WIDTH:, :], preferred_element_type=F32))
    hc = _rms(x1, gc_ref[...]).astype(BF16)
    qc = (jnp.dot(hc, wcq_ref[...], preferred_element_type=F32)
          * CROSS_HEAD_DIM ** -0.5).astype(BF16)
    for h in range(CROSS_HEADS):
        sl = slice(h * CROSS_HEAD_DIM, (h + 1) * CROSS_HEAD_DIM)
        kh = kv_ref[0, :, sl]
        vh = kv_ref[0, :, D_MODEL + h * CROSS_HEAD_DIM:D_MODEL + (h + 1) * CROSS_HEAD_DIM]
        s = lax.dot_general(qc[:, sl], kh, (((1,), (1,)), ((), ())),
                            preferred_element_type=F32)
        p = jnp.exp(s - jnp.max(s, axis=1, keepdims=True))
        l = jnp.sum(p, axis=1, keepdims=True)
        oh = jnp.dot(p.astype(BF16), vh, preferred_element_type=F32) / l
        oc_ref[:, sl] = oh.astype(BF16)
    x2 = x1 + jnp.dot(oc_ref[...], wco_ref[...], preferred_element_type=F32)

    hm = _rms(x2, gm_ref[...]).astype(BF16)
    acc = x2
    for c in range(D_FF // FF_CHUNK):
        sl = slice(c * FF_CHUNK, (c + 1) * FF_CHUNK)
        up = jnp.dot(hm, wup_ref[:, sl], preferred_element_type=F32)
        a = jnp.square(jnp.maximum(up, 0.0)).astype(BF16)
        acc = acc + jnp.dot(a, wdn_ref[sl, :], preferred_element_type=F32)
    o_ref[0] = _rms(acc, gf_ref[...])


def _post(x, mp, ya, w_out, g_cross, w_cq, kv, w_co, g_mlp, w_up, w_down, g_final):
    B, S, D = x.shape
    ts = ROW_TILE
    M = kv.shape[1]
    row = lambda b, s: (b, s, 0)
    const2 = lambda b, s: (0, 0)
    resident = lambda shape: pl.BlockSpec(shape, const2, pipeline_mode=pl.Buffered(1))
    return pl.pallas_call(
        _post_kernel,
        grid=(B, S // ts),
        in_specs=[
            pl.BlockSpec((1, ts, D), row),
            pl.BlockSpec((1, ts, POOL_WIDTH), row),
            pl.BlockSpec((1, DIFF_HEADS, ts, HEAD_WIDTH), lambda b, s: (b, 0, s, 0)),
            resident((D, D)),
            resident((1, D)),
            resident((D, D)),
            pl.BlockSpec((1, M, 2 * D), lambda b, s: (b, 0, 0)),
            resident((D, D)),
            resident((1, D)),
            resident((D, D_FF)),
            resident((D_FF, D)),
            resident((1, D)),
        ],
        out_specs=pl.BlockSpec((1, ts, D), row),
        out_shape=jax.ShapeDtypeStruct((B, S, D), F32),
        scratch_shapes=[pltpu.VMEM((ts, D), BF16)],
        compiler_params=pltpu.CompilerParams(
            dimension_semantics=("arbitrary", "arbitrary"),
            vmem_limit_bytes=VMEM_LIMIT),
        name="post",
    )(x, mp, ya, w_out, g_cross, w_cq, kv, w_co, g_mlp, w_up, w_down, g_final)


def _rope_tables(S):
    dh = DIFF_HEAD_DIM
    inv_freq = ROPE_THETA ** (-jnp.arange(0, dh, 2, dtype=F32) / dh)
    inv_freq = jnp.tile(inv_freq, LANES // (dh // 2))
    sign = jnp.where((jnp.arange(LANES) % dh) < dh // 2, -1.0, 1.0).astype(F32)
    assert S % LANES == 0
    ang_hi = (jnp.arange(S // LANES, dtype=F32) * LANES)[:, None] * inv_freq[None, :]
    ang_lo = jnp.arange(LANES, dtype=F32)[:, None] * inv_freq[None, :]
    ch, sh, cl, sl = lax.optimization_barrier(
        (jnp.cos(ang_hi), jnp.sin(ang_hi), jnp.cos(ang_lo), jnp.sin(ang_lo)))
    ch, sh, cl, sl = ch[:, None, :], sh[:, None, :], cl[None, :, :], sl[None, :, :]
    cos_t = (ch * cl - sh * sl).reshape(S, LANES)
    sin_t = ((sh * cl + ch * sl) * sign).reshape(S, LANES)
    return cos_t, sin_t


def kernel(x, mem, g_mix, w_in, w_pool, pool_scale, lambda_q1, lambda_k1, lambda_q2,
           lambda_k2, g_subln, w_out, g_cross, g_mem, w_cq, w_ckv, w_co, g_mlp, w_up,
           w_down, g_final):
    B, S, _ = x.shape
    assert g_mix.shape[0] == 1, "single-layer block"
    cos_t, sin_t = _rope_tables(S)
    mp, q, k, vt, ga = _in_proj(x, g_mix, w_in[0].astype(BF16), w_pool[0].astype(BF16),
                                pool_scale, cos_t, sin_t)
    kv = _mem_kv(mem, g_mem, w_ckv[0].astype(BF16))
    ya = _diff_attn(q, k, vt, ga, g_subln.reshape(HEAD_WIDTH, 1), lambda_q1, lambda_k1,
                    lambda_q2, lambda_k2)
    return _post(x, mp, ya, w_out[0].astype(BF16), g_cross, w_cq[0].astype(BF16), kv,
                 w_co[0].astype(BF16), g_mlp, w_up[0].astype(BF16), w_down[0].astype(BF16),
                 g_final.reshape(1, D_MODEL))
```

```python
import math

import jax
import jax.numpy as jnp
from jax import lax
from jax.experimental import pallas as pl
from jax.experimental.pallas import tpu as pltpu

D_MODEL = 1024
POOL_WIDTH = 512
POOL_WINDOWS = (2, 4, 8, 16)
POOL_GROUP_WIDTH = 128
POOL_HALO = 16
DIFF_HEADS = 4
DIFF_HEAD_DIM = 64
HEAD_WIDTH = 2 * DIFF_HEAD_DIM
Q_WIDTH = DIFF_HEADS * HEAD_WIDTH
Q_LO = POOL_WIDTH
K_LO = Q_LO + Q_WIDTH
V_LO = K_LO + Q_WIDTH
GATE_LO = V_LO + Q_WIDTH
IN_PROJ_WIDTH = GATE_LO + POOL_WIDTH + Q_WIDTH
ROPE_THETA = 10000.0
CROSS_HEADS = 4
CROSS_HEAD_DIM = 256
D_FF = 4096
NORM_EPS = 1e-6
NEG_BIG = -1e30
LAM_INIT = 0.8 - 0.6 * math.exp(-0.3 * 0)
LOG2E = math.log2(math.e)

LANES = 128
BF16_SUBLANES = 16
VMEM_LIMIT = 60 * 1024 * 1024

PROJ_TILE = 1024
ATTN_TQ = 512
ATTN_TK = 512
ATTN_UNROLL = 8
ROW_TILE = 1024
FF_CHUNK = 1024
VT_ROWS = HEAD_WIDTH + BF16_SUBLANES

BF16 = jnp.bfloat16
F32 = jnp.float32


def _rms(x, g):
    return x * lax.rsqrt(jnp.mean(x * x, axis=-1, keepdims=True) + NORM_EPS) * g


def _rotate_half_pairs(x):
    lane = lax.broadcasted_iota(jnp.int32, x.shape, 1)
    half = DIFF_HEAD_DIM // 2
    lower = (lane % DIFF_HEAD_DIM) < half
    return jnp.where(lower, pltpu.roll(x, LANES - half, axis=1), pltpu.roll(x, half, axis=1))


def _in_proj_kernel(x_ref, g_ref, w_ref, wpool_ref, pscale_ref, cos_ref, sin_ref,
                    mp_ref, q_ref, k_ref, vt_ref, ga_ref, ext_ref):
    si = pl.program_id(1)
    ts = x_ref.shape[1]
    h = _rms(x_ref[0], g_ref[...]).astype(BF16)

    def proj(lo, hi):
        return jnp.dot(h, w_ref[:, lo:hi], preferred_element_type=F32)

    gates = 1.0 / (1.0 + jnp.exp(-proj(GATE_LO, IN_PROJ_WIDTH)))
    for hd in range(DIFF_HEADS):
        lo = POOL_WIDTH + hd * HEAD_WIDTH
        ga_ref[0, hd] = gates[:, lo:lo + HEAD_WIDTH].astype(BF16)

    @pl.when(si == 0)
    def _():
        ext_ref[0:POOL_HALO, :] = jnp.zeros((POOL_HALO, POOL_WIDTH), F32)

    ext_ref[POOL_HALO:, :] = proj(0, POOL_WIDTH)
    q = proj(Q_LO, K_LO)
    k = proj(K_LO, V_LO)
    v = proj(V_LO, GATE_LO)

    pos = si * ts + lax.broadcasted_iota(jnp.int32, (ts, 1), 0)
    zs = []
    for g, w in enumerate(POOL_WINDOWS):
        lo, hi = g * POOL_GROUP_WIDTH, (g + 1) * POOL_GROUP_WIDTH
        e = ext_ref[:, lo:hi]
        win = e
        shift = 1
        while shift < w:
            win = win + pltpu.roll(win, shift, axis=0)
            shift *= 2
        cnt = jnp.minimum(pos + 1, w).astype(F32)
        zs.append((win[POOL_HALO:] / cnt - e[POOL_HALO:]).astype(BF16))
    ext_ref[0:POOL_HALO, :] = ext_ref[ts:ts + POOL_HALO, :]

    cos = cos_ref[...]
    sin = sin_ref[...]
    qscale = DIFF_HEAD_DIM ** -0.5 * LOG2E
    for c in range(Q_WIDTH // LANES):
        sl = slice(c * LANES, (c + 1) * LANES)
        qc = q[:, sl]
        kc = k[:, sl]
        q_ref[0, c] = ((qc * cos + _rotate_half_pairs(qc) * sin) * qscale).astype(BF16)
        k_ref[0, c] = (kc * cos + _rotate_half_pairs(kc) * sin).astype(BF16)

    for hd in range(DIFF_HEADS):
        vt_ref[0, hd] = v[:, hd * HEAD_WIDTH:(hd + 1) * HEAD_WIDTH].T.astype(BF16)

    for g in range(len(POOL_WINDOWS)):
        lo, hi = g * POOL_GROUP_WIDTH, (g + 1) * POOL_GROUP_WIDTH
        y = jnp.dot(zs[g], wpool_ref[g], preferred_element_type=F32)
        y = y * pscale_ref[:, lo:hi] * gates[:, lo:hi]
        mp_ref[0, :, lo:hi] = y.astype(BF16)


def _in_proj(x, g_mix, w_in, w_pool, pool_scale, cos_t, sin_t):
    B, S, D = x.shape
    ts = PROJ_TILE
    row = lambda b, s: (b, s, 0)
    const2 = lambda b, s: (0, 0)
    half = pl.BlockSpec((1, ts, POOL_WIDTH), row)
    half_sds = jax.ShapeDtypeStruct((B, S, POOL_WIDTH), BF16)
    heads = pl.BlockSpec((1, DIFF_HEADS, ts, HEAD_WIDTH), lambda b, s: (b, 0, s, 0))
    heads_sds = jax.ShapeDtypeStruct((B, DIFF_HEADS, S, HEAD_WIDTH), BF16)
    vt_spec = pl.BlockSpec((1, DIFF_HEADS, HEAD_WIDTH, ts), lambda b, s: (b, 0, 0, s))
    vt_sds = jax.ShapeDtypeStruct((B, DIFF_HEADS, HEAD_WIDTH, S), BF16)
    return pl.pallas_call(
        _in_proj_kernel,
        grid=(B, S // ts),
        in_specs=[
            pl.BlockSpec((1, ts, D), row),
            pl.BlockSpec((1, D), const2),
            pl.BlockSpec((D, IN_PROJ_WIDTH), const2),
            pl.BlockSpec((len(POOL_WINDOWS), POOL_GROUP_WIDTH, POOL_GROUP_WIDTH),
                         lambda b, s: (0, 0, 0)),
            pl.BlockSpec((1, POOL_WIDTH), const2),
            pl.BlockSpec((ts, LANES), lambda b, s: (s, 0)),
            pl.BlockSpec((ts, LANES), lambda b, s: (s, 0)),
        ],
        out_specs=[half, heads, heads, vt_spec, heads],
        out_shape=[half_sds, heads_sds, heads_sds, vt_sds, heads_sds],
        scratch_shapes=[pltpu.VMEM((ts + POOL_HALO, POOL_WIDTH), F32)],
        compiler_params=pltpu.CompilerParams(
            dimension_semantics=("arbitrary", "arbitrary"),
            vmem_limit_bytes=VMEM_LIMIT),
        name="in_proj",
    )(x, g_mix, w_in, w_pool, pool_scale, cos_t, sin_t)


def _mem_kv_kernel(mem_ref, g_ref, w_ref, kv_ref):
    mn = _rms(mem_ref[0], g_ref[...]).astype(BF16)
    kv_ref[0] = jnp.dot(mn, w_ref[...], preferred_element_type=F32).astype(BF16)


def _mem_kv(mem, g_mem, w_ckv):
    B, M, D = mem.shape
    return pl.pallas_call(
        _mem_kv_kernel,
        grid=(B,),
        in_specs=[
            pl.BlockSpec((1, M, D), lambda b: (b, 0, 0)),
            pl.BlockSpec((1, D), lambda b: (0, 0)),
            pl.BlockSpec((D, 2 * D), lambda b: (0, 0)),
        ],
        out_specs=pl.BlockSpec((1, M, 2 * D), lambda b: (b, 0, 0)),
        out_shape=jax.ShapeDtypeStruct((B, M, 2 * D), BF16),
        compiler_params=pltpu.CompilerParams(
            dimension_semantics=("arbitrary",), vmem_limit_bytes=VMEM_LIMIT),
        name="mem_kv",
    )(mem, g_mem, w_ckv)


def _diff_attn_kernel(q_ref, k_ref, vt_ref, ga_ref, gsub_ref, lq1_ref, lk1_ref,
                      lq2_ref, lk2_ref, o_ref, qs_ref, vx_ref, m_ref, acc_ref, st0_ref, st1_ref):
    tq = ATTN_TQ
    tk = ATTN_TK
    nkb = vx_ref.shape[0]
    nq = q_ref.shape[2] // tq

    for j in range(nkb):
        vx_ref[j, 0:HEAD_WIDTH, :] = vt_ref[0, 0, :, j * tk:(j + 1) * tk]
        vx_ref[j, HEAD_WIDTH:, :] = jnp.ones((BF16_SUBLANES, tk), BF16)

    lam = (jnp.exp(jnp.sum(lq1_ref[...] * lk1_ref[...], keepdims=True))
           - jnp.exp(jnp.sum(lq2_ref[...] * lk2_ref[...], keepdims=True)) + LAM_INIT)

    def scores(kb, c):
        start = pl.multiple_of(kb * tk, tk)
        return jnp.dot(k_ref[0, 0, pl.ds(start, tk), :], qs_ref[:, c * tq:(c + 1) * tq],
                       preferred_element_type=F32)

    def softmax(st, c, masked):
        cols = slice(c * tq, (c + 1) * tq)
        if masked:
            key = lax.broadcasted_iota(jnp.int32, st.shape, 0)
            qry = lax.broadcasted_iota(jnp.int32, st.shape, 1)
            st = jnp.where(key <= qry, st, NEG_BIG)
        m_prev = m_ref[:, cols]
        m_new = jnp.maximum(m_prev, jnp.max(st, axis=0, keepdims=True))
        m_ref[:, cols] = m_new
        return jnp.exp2(m_prev - m_new), jnp.exp2(st - m_new).astype(BF16)

    def pv(kb, c, alpha, pt):
        cols = slice(c * tq, (c + 1) * tq)
        acc_ref[:, cols] = alpha * acc_ref[:, cols] + jnp.dot(
            vx_ref[kb], pt, preferred_element_type=F32)

    def block(kb, masked):
        st1_ref[...] = scores(kb, 1)
        a0, p0 = softmax(st0_ref[...], 0, masked)
        pv(kb, 0, a0, p0)
        if not masked:
            st0_ref[...] = scores(kb + 1, 0)
        a1, p1 = softmax(st1_ref[...], 1, masked)
        pv(kb, 1, a1, p1)

    def group(base, n):
        for j in range(n):
            block(base + j, False)

    def rows_of(qi):
        return pl.ds(pl.multiple_of(qi * tq, tq), tq)

    def start_tile(qi):
        qt = q_ref[0, 0, rows_of(qi), :].astype(F32).T
        first = lax.broadcasted_iota(jnp.int32, qt.shape, 0) < DIFF_HEAD_DIM
        qs_ref[:, 0:tq] = jnp.where(first, qt, 0.0).astype(BF16)
        qs_ref[:, tq:] = jnp.where(first, 0.0, qt).astype(BF16)
        m_ref[...] = jnp.full(m_ref.shape, NEG_BIG, F32)
        acc_ref[...] = jnp.zeros(acc_ref.shape, F32)
        st0_ref[...] = scores(0, 0)

    def key_blocks(qi):
        def body(u, carry):
            group(ATTN_UNROLL * u, ATTN_UNROLL)
            return carry

        lax.fori_loop(0, qi // ATTN_UNROLL, body, 0)
        half = ATTN_UNROLL // 2
        done = (qi // ATTN_UNROLL) * ATTN_UNROLL
        take_half = (qi - done) >= half

        @pl.when(take_half)
        def _():
            group(done, half)

        done = done + jnp.where(take_half, half, 0)
        for r in range(half):
            @pl.when(qi - done == r)
            def _(r=r):
                group(done, r)
                block(qi, True)

    def finish_tile(qi):
        o1 = acc_ref[0:HEAD_WIDTH, 0:tq] / acc_ref[HEAD_WIDTH:HEAD_WIDTH + 1, 0:tq]
        o2 = acc_ref[0:HEAD_WIDTH, tq:] / acc_ref[HEAD_WIDTH:HEAD_WIDTH + 1, tq:]
        ot = o1 - lam * o2
        inv = lax.rsqrt(jnp.mean(ot * ot, axis=0, keepdims=True) + NORM_EPS)
        yt = ot * inv * gsub_ref[...] * (1.0 - LAM_INIT)
        rows = rows_of(qi)
        o_ref[0, 0, rows, :] = (yt.T * ga_ref[0, 0, rows, :].astype(F32)).astype(BF16)

    start_tile(0)
    block(0, True)

    def tile(qi, carry):
        finish_tile(qi - 1)
        start_tile(qi)
        key_blocks(qi)
        return carry

    lax.fori_loop(1, nq, tile, 0)
    finish_tile(nq - 1)


def _diff_attn(q, k, vt, ga, g_subln_col, lq1, lk1, lq2, lk2):
    B, _, S, _ = q.shape
    tq = ATTN_TQ
    assert tq == ATTN_TK and S % tq == 0
    seq = pl.BlockSpec((1, 1, S, HEAD_WIDTH), lambda b, h: (b, h, 0, 0))
    vtblk = pl.BlockSpec((1, 1, HEAD_WIDTH, S), lambda b, h: (b, h, 0, 0))
    vec = lambda n: pl.BlockSpec((1, n), lambda b, h: (0, 0))
    return pl.pallas_call(
        _diff_attn_kernel,
        grid=(B, DIFF_HEADS),
        in_specs=[seq, seq, vtblk, seq,
                  pl.BlockSpec((HEAD_WIDTH, 1), lambda b, h: (0, 0)),
                  vec(DIFF_HEAD_DIM), vec(DIFF_HEAD_DIM), vec(DIFF_HEAD_DIM), vec(DIFF_HEAD_DIM)],
        out_specs=seq,
        out_shape=jax.ShapeDtypeStruct((B, DIFF_HEADS, S, HEAD_WIDTH), BF16),
        scratch_shapes=[
            pltpu.VMEM((HEAD_WIDTH, 2 * tq), BF16),
            pltpu.VMEM((S // ATTN_TK, VT_ROWS, ATTN_TK), BF16),
            pltpu.VMEM((1, 2 * tq), F32),
            pltpu.VMEM((VT_ROWS, 2 * tq), F32),
            pltpu.VMEM((ATTN_TK, tq), F32),
            pltpu.VMEM((ATTN_TK, tq), F32),
        ],
        compiler_params=pltpu.CompilerParams(
            dimension_semantics=("arbitrary", "arbitrary"),
            vmem_limit_bytes=VMEM_LIMIT),
        name="diff_attn",
    )(q, k, vt, ga, g_subln_col, lq1, lk1, lq2, lk2)


def _post_kernel(x_ref, mp_ref, ya_ref, wout_ref, gc_ref, wcq_ref, kv_ref, wco_ref,
                 gm_ref, wup_ref, wdn_ref, gf_ref, o_ref, oc_ref):
    x1 = (x_ref[0]
          + jnp.dot(mp_ref[0], wout_ref[0:POOL_WIDTH, :], preferred_element_type=F32)
          + jnp.dot(jnp.concatenate([ya_ref[0, hd] for hd in range(DIFF_HEADS)], axis=1),
                    wout_ref[POOL_WIDTH:, :], preferred_element_type=F32))
    hc = _rms(x1, gc_ref[...]).astype(BF16)
    qc = (jnp.dot(hc, wcq_ref[...], preferred_element_type=F32)
          * CROSS_HEAD_DIM ** -0.5).astype(BF16)
    for h in range(CROSS_HEADS):
        sl = slice(h * CROSS_HEAD_DIM, (h + 1) * CROSS_HEAD_DIM)
        kh = kv_ref[0, :, sl]
        vh = kv_ref[0, :, D_MODEL + h * CROSS_HEAD_DIM:D_MODEL + (h + 1) * CROSS_HEAD_DIM]
        s = lax.dot_general(qc[:, sl], kh, (((1,), (1,)), ((), ())),
                            preferred_element_type=F32)
        p = jnp.exp(s - jnp.max(s, axis=1, keepdims=True))
        l = jnp.sum(p, axis=1, keepdims=True)
        oh = jnp.dot(p.astype(BF16), vh, preferred_element_type=F32) / l
        oc_ref[:, sl] = oh.astype(BF16)
    x2 = x1 + jnp.dot(oc_ref[...], wco_ref[...], preferred_element_type=F32)

    hm = _rms(x2, gm_ref[...]).astype(BF16)
    acc = x2
    for c in range(D_FF // FF_CHUNK):
        sl = slice(c * FF_CHUNK, (c + 1) * FF_CHUNK)
        up = jnp.dot(hm, wup_ref[:, sl], preferred_element_type=F32)
        a = jnp.square(jnp.maximum(up, 0.0)).astype(BF16)
        acc = acc + jnp.dot(a, wdn_ref[sl, :], preferred_element_type=F32)
    o_ref[0] = _rms(acc, gf_ref[...])


def _post(x, mp, ya, w_out, g_cross, w_cq, kv, w_co, g_mlp, w_up, w_down, g_final):
    B, S, D = x.shape
    ts = ROW_TILE
    M = kv.shape[1]
    row = lambda b, s: (b, s, 0)
    const2 = lambda b, s: (0, 0)
    resident = lambda shape: pl.BlockSpec(shape, const2, pipeline_mode=pl.Buffered(1))
    return pl.pallas_call(
        _post_kernel,
        grid=(B, S // ts),
        in_specs=[
            pl.BlockSpec((1, ts, D), row),
            pl.BlockSpec((1, ts, POOL_WIDTH), row),
            pl.BlockSpec((1, DIFF_HEADS, ts, HEAD_WIDTH), lambda b, s: (b, 0, s, 0)),
            resident((D, D)),
            resident((1, D)),
            resident((D, D)),
            pl.BlockSpec((1, M, 2 * D), lambda b, s: (b, 0, 0)),
            resident((D, D)),
            resident((1, D)),
            resident((D, D_FF)),
            resident((D_FF, D)),
            resident((1, D)),
        ],
        out_specs=pl.BlockSpec((1, ts, D), row),
        out_shape=jax.ShapeDtypeStruct((B, S, D), F32),
        scratch_shapes=[pltpu.VMEM((ts, D), BF16)],
        compiler_params=pltpu.CompilerParams(
            dimension_semantics=("arbitrary", "arbitrary"),
            vmem_limit_bytes=VMEM_LIMIT),
        name="post",
    )(x, mp, ya, w_out, g_cross, w_cq, kv, w_co, g_mlp, w_up, w_down, g_final)


def _rope_tables(S):
    dh = DIFF_HEAD_DIM
    inv_freq = ROPE_THETA ** (-jnp.arange(0, dh, 2, dtype=F32) / dh)
    inv_freq = jnp.tile(inv_freq, LANES // (dh // 2))
    sign = jnp.where((jnp.arange(LANES) % dh) < dh // 2, -1.0, 1.0).astype(F32)
    assert S % LANES == 0
    ang_hi = (jnp.arange(S // LANES, dtype=F32) * LANES)[:, None] * inv_freq[None, :]
    ang_lo = jnp.arange(LANES, dtype=F32)[:, None] * inv_freq[None, :]
    ch, sh, cl, sl = lax.optimization_barrier(
        (jnp.cos(ang_hi), jnp.sin(ang_hi), jnp.cos(ang_lo), jnp.sin(ang_lo)))
    ch, sh, cl, sl = ch[:, None, :], sh[:, None, :], cl[None, :, :], sl[None, :, :]
    cos_t = (ch * cl - sh * sl).reshape(S, LANES)
    sin_t = ((sh * cl + ch * sl) * sign).reshape(S, LANES)
    return cos_t, sin_t


def kernel(x, mem, g_mix, w_in, w_pool, pool_scale, lambda_q1, lambda_k1, lambda_q2,
           lambda_k2, g_subln, w_out, g_cross, g_mem, w_cq, w_ckv, w_co, g_mlp, w_up,
           w_down, g_final):
    B, S, _ = x.shape
    assert g_mix.shape[0] == 1, "single-layer block"
    cos_t, sin_t = _rope_tables(S)
    mp, q, k, vt, ga = _in_proj(x, g_mix, w_in[0].astype(BF16), w_pool[0].astype(BF16),
                                pool_scale, cos_t, sin_t)
    kv = _mem_kv(mem, g_mem, w_ckv[0].astype(BF16))
    ya = _diff_attn(q, k, vt, ga, g_subln.reshape(HEAD_WIDTH, 1), lambda_q1, lambda_k1,
                    lambda_q2, lambda_k2)
    return _post(x, mp, ya, w_out[0].astype(BF16), g_cross, w_cq[0].astype(BF16), kv,
                 w_co[0].astype(BF16), g_mlp, w_up[0].astype(BF16), w_down[0].astype(BF16),
                 g_final.reshape(1, D_MODEL))
```

```python
import math

import jax
import jax.numpy as jnp
from jax import lax
from jax.experimental import pallas as pl
from jax.experimental.pallas import tpu as pltpu

D_MODEL = 1024
POOL_WIDTH = 512
POOL_WINDOWS = (2, 4, 8, 16)
POOL_GROUP_WIDTH = 128
POOL_HALO = 16
DIFF_HEADS = 4
DIFF_HEAD_DIM = 64
HEAD_WIDTH = 2 * DIFF_HEAD_DIM
Q_WIDTH = DIFF_HEADS * HEAD_WIDTH
Q_LO = POOL_WIDTH
K_LO = Q_LO + Q_WIDTH
V_LO = K_LO + Q_WIDTH
GATE_LO = V_LO + Q_WIDTH
IN_PROJ_WIDTH = GATE_LO + POOL_WIDTH + Q_WIDTH
ROPE_THETA = 10000.0
CROSS_HEADS = 4
CROSS_HEAD_DIM = 256
D_FF = 4096
NORM_EPS = 1e-6
NEG_BIG = -1e30
LAM_INIT = 0.8 - 0.6 * math.exp(-0.3 * 0)
LOG2E = math.log2(math.e)

LANES = 128
BF16_SUBLANES = 16
VMEM_LIMIT = 60 * 1024 * 1024

PROJ_TILE = 1024
ATTN_TQ = 512
ATTN_TK = 512
ATTN_UNROLL = 8
ROW_TILE = 1024
FF_CHUNK = 1024
VT_ROWS = HEAD_WIDTH + BF16_SUBLANES

BF16 = jnp.bfloat16
F32 = jnp.float32


def _rms(x, g):
    return x * lax.rsqrt(jnp.mean(x * x, axis=-1, keepdims=True) + NORM_EPS) * g


def _rotate_half_pairs(x):
    lane = lax.broadcasted_iota(jnp.int32, x.shape, 1)
    half = DIFF_HEAD_DIM // 2
    lower = (lane % DIFF_HEAD_DIM) < half
    return jnp.where(lower, pltpu.roll(x, LANES - half, axis=1), pltpu.roll(x, half, axis=1))


def _in_proj_kernel(x_ref, g_ref, w_ref, wpool_ref, pscale_ref, cos_ref, sin_ref,
                    mp_ref, q_ref, k_ref, vt_ref, ga_ref, ext_ref):
    si = pl.program_id(1)
    ts = x_ref.shape[1]
    h = _rms(x_ref[0], g_ref[...]).astype(BF16)

    def proj(lo, hi):
        return jnp.dot(h, w_ref[:, lo:hi], preferred_element_type=F32)

    gates = 0.5 + 0.5 * jnp.tanh(0.5 * proj(GATE_LO, IN_PROJ_WIDTH))
    for hd in range(DIFF_HEADS):
        lo = POOL_WIDTH + hd * HEAD_WIDTH
        ga_ref[0, hd] = gates[:, lo:lo + HEAD_WIDTH].astype(BF16)

    @pl.when(si == 0)
    def _():
        ext_ref[0:POOL_HALO, :] = jnp.zeros((POOL_HALO, POOL_WIDTH), F32)

    ext_ref[POOL_HALO:, :] = proj(0, POOL_WIDTH)
    q = proj(Q_LO, K_LO)
    k = proj(K_LO, V_LO)
    v = proj(V_LO, GATE_LO)

    pos = si * ts + lax.broadcasted_iota(jnp.int32, (ts, 1), 0)
    zs = []
    for g, w in enumerate(POOL_WINDOWS):
        lo, hi = g * POOL_GROUP_WIDTH, (g + 1) * POOL_GROUP_WIDTH
        e = ext_ref[:, lo:hi]
        win = e
        shift = 1
        while shift < w:
            win = win + pltpu.roll(win, shift, axis=0)
            shift *= 2
        cnt = jnp.minimum(pos + 1, w).astype(F32)
        zs.append((win[POOL_HALO:] / cnt - e[POOL_HALO:]).astype(BF16))
    ext_ref[0:POOL_HALO, :] = ext_ref[ts:ts + POOL_HALO, :]

    cos = cos_ref[...]
    sin = sin_ref[...]
    qscale = DIFF_HEAD_DIM ** -0.5 * LOG2E
    for c in range(Q_WIDTH // LANES):
        sl = slice(c * LANES, (c + 1) * LANES)
        qc = q[:, sl]
        kc = k[:, sl]
        q_ref[0, c] = ((qc * cos + _rotate_half_pairs(qc) * sin) * qscale).astype(BF16)
        k_ref[0, c] = (kc * cos + _rotate_half_pairs(kc) * sin).astype(BF16)

    for hd in range(DIFF_HEADS):
        vt_ref[0, hd] = v[:, hd * HEAD_WIDTH:(hd + 1) * HEAD_WIDTH].T.astype(BF16)

    for g in range(len(POOL_WINDOWS)):
        lo, hi = g * POOL_GROUP_WIDTH, (g + 1) * POOL_GROUP_WIDTH
        y = jnp.dot(zs[g], wpool_ref[g], preferred_element_type=F32)
        y = y * pscale_ref[:, lo:hi] * gates[:, lo:hi]
        mp_ref[0, :, lo:hi] = y.astype(BF16)


def _in_proj(x, g_mix, w_in, w_pool, pool_scale, cos_t, sin_t):
    B, S, D = x.shape
    ts = PROJ_TILE
    row = lambda b, s: (b, s, 0)
    const2 = lambda b, s: (0, 0)
    half = pl.BlockSpec((1, ts, POOL_WIDTH), row)
    half_sds = jax.ShapeDtypeStruct((B, S, POOL_WIDTH), BF16)
    heads = pl.BlockSpec((1, DIFF_HEADS, ts, HEAD_WIDTH), lambda b, s: (b, 0, s, 0))
    heads_sds = jax.ShapeDtypeStruct((B, DIFF_HEADS, S, HEAD_WIDTH), BF16)
    vt_spec = pl.BlockSpec((1, DIFF_HEADS, HEAD_WIDTH, ts), lambda b, s: (b, 0, 0, s))
    vt_sds = jax.ShapeDtypeStruct((B, DIFF_HEADS, HEAD_WIDTH, S), BF16)
    return pl.pallas_call(
        _in_proj_kernel,
        grid=(B, S // ts),
        in_specs=[
            pl.BlockSpec((1, ts, D), row),
            pl.BlockSpec((1, D), const2),
            pl.BlockSpec((D, IN_PROJ_WIDTH), const2),
            pl.BlockSpec((len(POOL_WINDOWS), POOL_GROUP_WIDTH, POOL_GROUP_WIDTH),
                         lambda b, s: (0, 0, 0)),
            pl.BlockSpec((1, POOL_WIDTH), const2),
            pl.BlockSpec((ts, LANES), lambda b, s: (s, 0)),
            pl.BlockSpec((ts, LANES), lambda b, s: (s, 0)),
        ],
        out_specs=[half, heads, heads, vt_spec, heads],
        out_shape=[half_sds, heads_sds, heads_sds, vt_sds, heads_sds],
        scratch_shapes=[pltpu.VMEM((ts + POOL_HALO, POOL_WIDTH), F32)],
        compiler_params=pltpu.CompilerParams(
            dimension_semantics=("arbitrary", "arbitrary"),
            vmem_limit_bytes=VMEM_LIMIT),
        name="in_proj",
    )(x, g_mix, w_in, w_pool, pool_scale, cos_t, sin_t)


def _mem_kv_kernel(mem_ref, g_ref, w_ref, kv_ref):
    mn = _rms(mem_ref[0], g_ref[...]).astype(BF16)
    kv_ref[0] = jnp.dot(mn, w_ref[...], preferred_element_type=F32).astype(BF16)


def _mem_kv(mem, g_mem, w_ckv):
    B, M, D = mem.shape
    return pl.pallas_call(
        _mem_kv_kernel,
        grid=(B,),
        in_specs=[
            pl.BlockSpec((1, M, D), lambda b: (b, 0, 0)),
            pl.BlockSpec((1, D), lambda b: (0, 0)),
            pl.BlockSpec((D, 2 * D), lambda b: (0, 0)),
        ],
        out_specs=pl.BlockSpec((1, M, 2 * D), lambda b: (b, 0, 0)),
        out_shape=jax.ShapeDtypeStruct((B, M, 2 * D), BF16),
        compiler_params=pltpu.CompilerParams(
            dimension_semantics=("arbitrary",), vmem_limit_bytes=VMEM_LIMIT),
        name="mem_kv",
    )(mem, g_mem, w_ckv)


def _diff_attn_kernel(q_ref, k_ref, vt_ref, ga_ref, gsub_ref, lq1_ref, lk1_ref,
                      lq2_ref, lk2_ref, o_ref, qs_ref, vx_ref, m_ref, acc_ref, st0_ref, st1_ref):
    tq = ATTN_TQ
    tk = ATTN_TK
    nkb = vx_ref.shape[0]
    nq = q_ref.shape[2] // tq

    for j in range(nkb):
        vx_ref[j, 0:HEAD_WIDTH, :] = vt_ref[0, 0, :, j * tk:(j + 1) * tk]
        vx_ref[j, HEAD_WIDTH:, :] = jnp.ones((BF16_SUBLANES, tk), BF16)

    lam = (jnp.exp(jnp.sum(lq1_ref[...] * lk1_ref[...], keepdims=True))
           - jnp.exp(jnp.sum(lq2_ref[...] * lk2_ref[...], keepdims=True)) + LAM_INIT)

    def scores(kb, c):
        start = pl.multiple_of(kb * tk, tk)
        return jnp.dot(k_ref[0, 0, pl.ds(start, tk), :], qs_ref[:, c * tq:(c + 1) * tq],
                       preferred_element_type=F32)

    def softmax(st, c, masked):
        cols = slice(c * tq, (c + 1) * tq)
        if masked:
            key = lax.broadcasted_iota(jnp.int32, st.shape, 0)
            qry = lax.broadcasted_iota(jnp.int32, st.shape, 1)
            st = jnp.where(key <= qry, st, NEG_BIG)
        m_prev = m_ref[:, cols]
        m_new = jnp.maximum(m_prev, jnp.max(st, axis=0, keepdims=True))
        m_ref[:, cols] = m_new
        return jnp.exp2(m_prev - m_new), jnp.exp2(st - m_new).astype(BF16)

    def pv(kb, c, alpha, pt):
        cols = slice(c * tq, (c + 1) * tq)
        acc_ref[:, cols] = alpha * acc_ref[:, cols] + jnp.dot(
            vx_ref[kb], pt, preferred_element_type=F32)

    def block(kb, masked):
        st1_ref[...] = scores(kb, 1)
        a0, p0 = softmax(st0_ref[...], 0, masked)
        pv(kb, 0, a0, p0)
        if not masked:
            st0_ref[...] = scores(kb + 1, 0)
        a1, p1 = softmax(st1_ref[...], 1, masked)
        pv(kb, 1, a1, p1)

    def group(base, n):
        for j in range(n):
            block(base + j, False)

    def rows_of(qi):
        return pl.ds(pl.multiple_of(qi * tq, tq), tq)

    def start_tile(qi):
        qt = q_ref[0, 0, rows_of(qi), :].astype(F32).T
        first = lax.broadcasted_iota(jnp.int32, qt.shape, 0) < DIFF_HEAD_DIM
        qs_ref[:, 0:tq] = jnp.where(first, qt, 0.0).astype(BF16)
        qs_ref[:, tq:] = jnp.where(first, 0.0, qt).astype(BF16)
        m_ref[...] = jnp.full(m_ref.shape, NEG_BIG, F32)
        acc_ref[...] = jnp.zeros(acc_ref.shape, F32)
        st0_ref[...] = scores(0, 0)

    def key_blocks(qi):
        def body(u, carry):
            group(ATTN_UNROLL * u, ATTN_UNROLL)
            return carry

        lax.fori_loop(0, qi // ATTN_UNROLL, body, 0)
        half = ATTN_UNROLL // 2
        done = (qi // ATTN_UNROLL) * ATTN_UNROLL
        take_half = (qi - done) >= half

        @pl.when(take_half)
        def _():
            group(done, half)

        done = done + jnp.where(take_half, half, 0)
        for r in range(half):
            @pl.when(qi - done == r)
            def _(r=r):
                group(done, r)
                block(qi, True)

    def finish_tile(qi):
        o1 = acc_ref[0:HEAD_WIDTH, 0:tq] / acc_ref[HEAD_WIDTH:HEAD_WIDTH + 1, 0:tq]
        o2 = acc_ref[0:HEAD_WIDTH, tq:] / acc_ref[HEAD_WIDTH:HEAD_WIDTH + 1, tq:]
        ot = o1 - lam * o2
        inv = lax.rsqrt(jnp.mean(ot * ot, axis=0, keepdims=True) + NORM_EPS)
        yt = ot * inv * gsub_ref[...] * (1.0 - LAM_INIT)
        rows = rows_of(qi)
        o_ref[0, 0, rows, :] = (yt.T * ga_ref[0, 0, rows, :].astype(F32)).astype(BF16)

    start_tile(0)
    block(0, True)

    def tile(qi, carry):
        finish_tile(qi - 1)
        start_tile(qi)
        key_blocks(qi)
        return carry

    lax.fori_loop(1, nq, tile, 0)
    finish_tile(nq - 1)


def _diff_attn(q, k, vt, ga, g_subln_col, lq1, lk1, lq2, lk2):
    B, _, S, _ = q.shape
    tq = ATTN_TQ
    assert tq == ATTN_TK and S % tq == 0
    seq = pl.BlockSpec((1, 1, S, HEAD_WIDTH), lambda b, h: (b, h, 0, 0))
    vtblk = pl.BlockSpec((1, 1, HEAD_WIDTH, S), lambda b, h: (b, h, 0, 0))
    vec = lambda n: pl.BlockSpec((1, n), lambda b, h: (0, 0))
    return pl.pallas_call(
        _diff_attn_kernel,
        grid=(B, DIFF_HEADS),
        in_specs=[seq, seq, vtblk, seq,
                  pl.BlockSpec((HEAD_WIDTH, 1), lambda b, h: (0, 0)),
                  vec(DIFF_HEAD_DIM), vec(DIFF_HEAD_DIM), vec(DIFF_HEAD_DIM), vec(DIFF_HEAD_DIM)],
        out_specs=seq,
        out_shape=jax.ShapeDtypeStruct((B, DIFF_HEADS, S, HEAD_WIDTH), BF16),
        scratch_shapes=[
            pltpu.VMEM((HEAD_WIDTH, 2 * tq), BF16),
            pltpu.VMEM((S // ATTN_TK, VT_ROWS, ATTN_TK), BF16),
            pltpu.VMEM((1, 2 * tq), F32),
            pltpu.VMEM((VT_ROWS, 2 * tq), F32),
            pltpu.VMEM((ATTN_TK, tq), F32),
            pltpu.VMEM((ATTN_TK, tq), F32),
        ],
        compiler_params=pltpu.CompilerParams(
            dimension_semantics=("arbitrary", "arbitrary"),
            vmem_limit_bytes=VMEM_LIMIT),
        name="diff_attn",
    )(q, k, vt, ga, g_subln_col, lq1, lk1, lq2, lk2)


def _post_kernel(x_ref, mp_ref, ya_ref, wout_ref, gc_ref, wcq_ref, kv_ref, wco_ref,
                 gm_ref, wup_ref, wdn_ref, gf_ref, o_ref, oc_ref):
    x1 = (x_ref[0]
          + jnp.dot(mp_ref[0], wout_ref[0:POOL_WIDTH, :], preferred_element_type=F32)
          + jnp.dot(jnp.concatenate([ya_ref[0, hd] for hd in range(DIFF_HEADS)], axis=1),
                    wout_ref[POOL_WIDTH:, :], preferred_element_type=F32))
    hc = _rms(x1, gc_ref[...]).astype(BF16)
    qc = (jnp.dot(hc, wcq_ref[...], preferred_element_type=F32)
          * CROSS_HEAD_DIM ** -0.5).astype(BF16)
    for h in range(CROSS_HEADS):
        sl = slice(h * CROSS_HEAD_DIM, (h + 1) * CROSS_HEAD_DIM)
        kh = kv_ref[0, :, sl]
        vh = kv_ref[0, :, D_MODEL + h * CROSS_HEAD_DIM:D_MODEL + (h + 1) * CROSS_HEAD_DIM]
        s = lax.dot_general(qc[:, sl], kh, (((1,), (1,)), ((), ())),
                            preferred_element_type=F32)
        p = jnp.exp(s - jnp.max(s, axis=1, keepdims=True))
        l = jnp.sum(p, axis=1, keepdims=True)
        oh = jnp.dot(p.astype(BF16), vh, preferred_element_type=F32) / l
        oc_ref[:, sl] = oh.astype(BF16)
    x2 = x1 + jnp.dot(oc_ref[...], wco_ref[...], preferred_element_type=F32)

    hm = _rms(x2, gm_ref[...]).astype(BF16)
    acc = x2
    for c in range(D_FF // FF_CHUNK):
        sl = slice(c * FF_CHUNK, (c + 1) * FF_CHUNK)
        up = jnp.dot(hm, wup_ref[:, sl], preferred_element_type=F32)
        a = jnp.square(jnp.maximum(up, 0.0)).astype(BF16)
        acc = acc + jnp.dot(a, wdn_ref[sl, :], preferred_element_type=F32)
    o_ref[0] = _rms(acc, gf_ref[...])


def _post(x, mp, ya, w_out, g_cross, w_cq, kv, w_co, g_mlp, w_up, w_down, g_final):
    B, S, D = x.shape
    ts = ROW_TILE
    M = kv.shape[1]
    row = lambda b, s: (b, s, 0)
    const2 = lambda b, s: (0, 0)
    resident = lambda shape: pl.BlockSpec(shape, const2, pipeline_mode=pl.Buffered(1))
    return pl.pallas_call(
        _post_kernel,
        grid=(B, S // ts),
        in_specs=[
            pl.BlockSpec((1, ts, D), row),
            pl.BlockSpec((1, ts, POOL_WIDTH), row),
            pl.BlockSpec((1, DIFF_HEADS, ts, HEAD_WIDTH), lambda b, s: (b, 0, s, 0)),
            resident((D, D)),
            resident((1, D)),
            resident((D, D)),
            pl.BlockSpec((1, M, 2 * D), lambda b, s: (b, 0, 0)),
            resident((D, D)),
            resident((1, D)),
            resident((D, D_FF)),
            resident((D_FF, D)),
            resident((1, D)),
        ],
        out_specs=pl.BlockSpec((1, ts, D), row),
        out_shape=jax.ShapeDtypeStruct((B, S, D), F32),
        scratch_shapes=[pltpu.VMEM((ts, D), BF16)],
        compiler_params=pltpu.CompilerParams(
            dimension_semantics=("arbitrary", "arbitrary"),
            vmem_limit_bytes=VMEM_LIMIT),
        name="post",
    )(x, mp, ya, w_out, g_cross, w_cq, kv, w_co, g_mlp, w_up, w_down, g_final)


def _rope_tables(S):
    dh = DIFF_HEAD_DIM
    inv_freq = ROPE_THETA ** (-jnp.arange(0, dh, 2, dtype=F32) / dh)
    inv_freq = jnp.tile(inv_freq, LANES // (dh // 2))
    sign = jnp.where((jnp.arange(LANES) % dh) < dh // 2, -1.0, 1.0).astype(F32)
    assert S % LANES == 0
    ang_hi = (jnp.arange(S // LANES, dtype=F32) * LANES)[:, None] * inv_freq[None, :]
    ang_lo = jnp.arange(LANES, dtype=F32)[:, None] * inv_freq[None, :]
    ch, sh, cl, sl = lax.optimization_barrier(
        (jnp.cos(ang_hi), jnp.sin(ang_hi), jnp.cos(ang_lo), jnp.sin(ang_lo)))
    ch, sh, cl, sl = ch[:, None, :], sh[:, None, :], cl[None, :, :], sl[None, :, :]
    cos_t = (ch * cl - sh * sl).reshape(S, LANES)
    sin_t = ((sh * cl + ch * sl) * sign).reshape(S, LANES)
    return cos_t, sin_t


def kernel(x, mem, g_mix, w_in, w_pool, pool_scale, lambda_q1, lambda_k1, lambda_q2,
           lambda_k2, g_subln, w_out, g_cross, g_mem, w_cq, w_ckv, w_co, g_mlp, w_up,
           w_down, g_final):
    B, S, _ = x.shape
    assert g_mix.shape[0] == 1, "single-layer block"
    cos_t, sin_t = _rope_tables(S)
    mp, q, k, vt, ga = _in_proj(x, g_mix, w_in[0].astype(BF16), w_pool[0].astype(BF16),
                                pool_scale, cos_t, sin_t)
    kv = _mem_kv(mem, g_mem, w_ckv[0].astype(BF16))
    ya = _diff_attn(q, k, vt, ga, g_subln.reshape(HEAD_WIDTH, 1), lambda_q1, lambda_k1,
                    lambda_q2, lambda_k2)
    return _post(x, mp, ya, w_out[0].astype(BF16), g_cross, w_cq[0].astype(BF16), kv,
                 w_co[0].astype(BF16), g_mlp, w_up[0].astype(BF16), w_down[0].astype(BF16),
                 g_final.reshape(1, D_MODEL))
```

```python
import math

import jax
import jax.numpy as jnp
from jax import lax
from jax.experimental import pallas as pl
from jax.experimental.pallas import tpu as pltpu

D_MODEL = 1024
POOL_WIDTH = 512
POOL_WINDOWS = (2, 4, 8, 16)
POOL_GROUP_WIDTH = 128
POOL_HALO = 16
DIFF_HEADS = 4
DIFF_HEAD_DIM = 64
HEAD_WIDTH = 2 * DIFF_HEAD_DIM
Q_WIDTH = DIFF_HEADS * HEAD_WIDTH
Q_LO = POOL_WIDTH
K_LO = Q_LO + Q_WIDTH
V_LO = K_LO + Q_WIDTH
GATE_LO = V_LO + Q_WIDTH
IN_PROJ_WIDTH = GATE_LO + POOL_WIDTH + Q_WIDTH
ROPE_THETA = 10000.0
CROSS_HEADS = 4
CROSS_HEAD_DIM = 256
D_FF = 4096
NORM_EPS = 1e-6
NEG_BIG = -1e30
LAM_INIT = 0.8 - 0.6 * math.exp(-0.3 * 0)
LOG2E = math.log2(math.e)

LANES = 128
BF16_SUBLANES = 16
VMEM_LIMIT = 60 * 1024 * 1024

PROJ_TILE = 1024
ATTN_TQ = 512
ATTN_TK = 512
ATTN_UNROLL = 16
ATTN_TAIL = 4
ROW_TILE = 1024
FF_CHUNK = 1024
VT_ROWS = HEAD_WIDTH + BF16_SUBLANES

BF16 = jnp.bfloat16
F32 = jnp.float32


def _rms(x, g):
    return x * lax.rsqrt(jnp.mean(x * x, axis=-1, keepdims=True) + NORM_EPS) * g


def _rotate_half_pairs(x):
    lane = lax.broadcasted_iota(jnp.int32, x.shape, 1)
    half = DIFF_HEAD_DIM // 2
    lower = (lane % DIFF_HEAD_DIM) < half
    return jnp.where(lower, pltpu.roll(x, LANES - half, axis=1), pltpu.roll(x, half, axis=1))


def _in_proj_kernel(x_ref, g_ref, w_ref, wpool_ref, pscale_ref, cos_ref, sin_ref,
                    mp_ref, q_ref, k_ref, vt_ref, ga_ref, ext_ref):
    si = pl.program_id(1)
    ts = x_ref.shape[1]
    h = _rms(x_ref[0], g_ref[...]).astype(BF16)

    def proj(lo, hi):
        return jnp.dot(h, w_ref[:, lo:hi], preferred_element_type=F32)

    gates = 0.5 + 0.5 * jnp.tanh(0.5 * proj(GATE_LO, IN_PROJ_WIDTH))
    for hd in range(DIFF_HEADS):
        lo = POOL_WIDTH + hd * HEAD_WIDTH
        ga_ref[0, hd] = gates[:, lo:lo + HEAD_WIDTH].astype(BF16)

    @pl.when(si == 0)
    def _():
        ext_ref[0:POOL_HALO, :] = jnp.zeros((POOL_HALO, POOL_WIDTH), F32)

    ext_ref[POOL_HALO:, :] = proj(0, POOL_WIDTH)
    q = proj(Q_LO, K_LO)
    k = proj(K_LO, V_LO)
    v = proj(V_LO, GATE_LO)

    pos = si * ts + lax.broadcasted_iota(jnp.int32, (ts, 1), 0)
    zs = []
    for g, w in enumerate(POOL_WINDOWS):
        lo, hi = g * POOL_GROUP_WIDTH, (g + 1) * POOL_GROUP_WIDTH
        e = ext_ref[:, lo:hi]
        win = e
        shift = 1
        while shift < w:
            win = win + pltpu.roll(win, shift, axis=0)
            shift *= 2
        cnt = jnp.minimum(pos + 1, w).astype(F32)
        zs.append((win[POOL_HALO:] / cnt - e[POOL_HALO:]).astype(BF16))
    ext_ref[0:POOL_HALO, :] = ext_ref[ts:ts + POOL_HALO, :]

    cos = cos_ref[...]
    sin = sin_ref[...]
    qscale = DIFF_HEAD_DIM ** -0.5 * LOG2E
    for c in range(Q_WIDTH // LANES):
        sl = slice(c * LANES, (c + 1) * LANES)
        qc = q[:, sl]
        kc = k[:, sl]
        q_ref[0, c] = ((qc * cos + _rotate_half_pairs(qc) * sin) * qscale).astype(BF16)
        k_ref[0, c] = (kc * cos + _rotate_half_pairs(kc) * sin).astype(BF16)

    for hd in range(DIFF_HEADS):
        vt_ref[0, hd] = v[:, hd * HEAD_WIDTH:(hd + 1) * HEAD_WIDTH].T.astype(BF16)

    for g in range(len(POOL_WINDOWS)):
        lo, hi = g * POOL_GROUP_WIDTH, (g + 1) * POOL_GROUP_WIDTH
        y = jnp.dot(zs[g], wpool_ref[g], preferred_element_type=F32)
        y = y * pscale_ref[:, lo:hi] * gates[:, lo:hi]
        mp_ref[0, :, lo:hi] = y.astype(BF16)


def _in_proj(x, g_mix, w_in, w_pool, pool_scale, cos_t, sin_t):
    B, S, D = x.shape
    ts = PROJ_TILE
    row = lambda b, s: (b, s, 0)
    const2 = lambda b, s: (0, 0)
    half = pl.BlockSpec((1, ts, POOL_WIDTH), row)
    half_sds = jax.ShapeDtypeStruct((B, S, POOL_WIDTH), BF16)
    heads = pl.BlockSpec((1, DIFF_HEADS, ts, HEAD_WIDTH), lambda b, s: (b, 0, s, 0))
    heads_sds = jax.ShapeDtypeStruct((B, DIFF_HEADS, S, HEAD_WIDTH), BF16)
    vt_spec = pl.BlockSpec((1, DIFF_HEADS, HEAD_WIDTH, ts), lambda b, s: (b, 0, 0, s))
    vt_sds = jax.ShapeDtypeStruct((B, DIFF_HEADS, HEAD_WIDTH, S), BF16)
    return pl.pallas_call(
        _in_proj_kernel,
        grid=(B, S // ts),
        in_specs=[
            pl.BlockSpec((1, ts, D), row),
            pl.BlockSpec((1, D), const2),
            pl.BlockSpec((D, IN_PROJ_WIDTH), const2),
            pl.BlockSpec((len(POOL_WINDOWS), POOL_GROUP_WIDTH, POOL_GROUP_WIDTH),
                         lambda b, s: (0, 0, 0)),
            pl.BlockSpec((1, POOL_WIDTH), const2),
            pl.BlockSpec((ts, LANES), lambda b, s: (s, 0)),
            pl.BlockSpec((ts, LANES), lambda b, s: (s, 0)),
        ],
        out_specs=[half, heads, heads, vt_spec, heads],
        out_shape=[half_sds, heads_sds, heads_sds, vt_sds, heads_sds],
        scratch_shapes=[pltpu.VMEM((ts + POOL_HALO, POOL_WIDTH), F32)],
        compiler_params=pltpu.CompilerParams(
            dimension_semantics=("arbitrary", "arbitrary"),
            vmem_limit_bytes=VMEM_LIMIT),
        name="in_proj",
    )(x, g_mix, w_in, w_pool, pool_scale, cos_t, sin_t)


def _mem_kv_kernel(mem_ref, g_ref, w_ref, kv_ref):
    mn = _rms(mem_ref[0], g_ref[...]).astype(BF16)
    kv_ref[0] = jnp.dot(mn, w_ref[...], preferred_element_type=F32).astype(BF16)


def _mem_kv(mem, g_mem, w_ckv):
    B, M, D = mem.shape
    return pl.pallas_call(
        _mem_kv_kernel,
        grid=(B,),
        in_specs=[
            pl.BlockSpec((1, M, D), lambda b: (b, 0, 0)),
            pl.BlockSpec((1, D), lambda b: (0, 0)),
            pl.BlockSpec((D, 2 * D), lambda b: (0, 0)),
        ],
        out_specs=pl.BlockSpec((1, M, 2 * D), lambda b: (b, 0, 0)),
        out_shape=jax.ShapeDtypeStruct((B, M, 2 * D), BF16),
        compiler_params=pltpu.CompilerParams(
            dimension_semantics=("arbitrary",), vmem_limit_bytes=VMEM_LIMIT),
        name="mem_kv",
    )(mem, g_mem, w_ckv)


def _diff_attn_kernel(q_ref, k_ref, vt_ref, ga_ref, gsub_ref, lq1_ref, lk1_ref,
                      lq2_ref, lk2_ref, o_ref, qs_ref, vx_ref, m_ref, acc_ref, st0_ref, st1_ref):
    tq = ATTN_TQ
    tk = ATTN_TK
    nkb = vx_ref.shape[0]
    nq = q_ref.shape[2] // tq

    for j in range(nkb):
        vx_ref[j, 0:HEAD_WIDTH, :] = vt_ref[0, 0, :, j * tk:(j + 1) * tk]
        vx_ref[j, HEAD_WIDTH:, :] = jnp.ones((BF16_SUBLANES, tk), BF16)

    lam = (jnp.exp(jnp.sum(lq1_ref[...] * lk1_ref[...], keepdims=True))
           - jnp.exp(jnp.sum(lq2_ref[...] * lk2_ref[...], keepdims=True)) + LAM_INIT)

    def scores(kb, c):
        start = pl.multiple_of(kb * tk, tk)
        return jnp.dot(k_ref[0, 0, pl.ds(start, tk), :], qs_ref[:, c * tq:(c + 1) * tq],
                       preferred_element_type=F32)

    def softmax(st, c, masked):
        cols = slice(c * tq, (c + 1) * tq)
        if masked:
            key = lax.broadcasted_iota(jnp.int32, st.shape, 0)
            qry = lax.broadcasted_iota(jnp.int32, st.shape, 1)
            st = jnp.where(key <= qry, st, NEG_BIG)
        m_prev = m_ref[:, cols]
        m_new = jnp.maximum(m_prev, jnp.max(st, axis=0, keepdims=True))
        m_ref[:, cols] = m_new
        return jnp.exp2(m_prev - m_new), jnp.exp2(st - m_new).astype(BF16)

    def pv(kb, c, alpha, pt):
        cols = slice(c * tq, (c + 1) * tq)
        acc_ref[:, cols] = alpha * acc_ref[:, cols] + jnp.dot(
            vx_ref[kb], pt, preferred_element_type=F32)

    def block(kb, masked):
        st1_ref[...] = scores(kb, 1)
        a0, p0 = softmax(st0_ref[...], 0, masked)
        pv(kb, 0, a0, p0)
        if not masked:
            st0_ref[...] = scores(kb + 1, 0)
        a1, p1 = softmax(st1_ref[...], 1, masked)
        pv(kb, 1, a1, p1)

    def group(base, n):
        for j in range(n):
            block(base + j, False)

    def rows_of(qi):
        return pl.ds(pl.multiple_of(qi * tq, tq), tq)

    def start_tile(qi):
        qt = q_ref[0, 0, rows_of(qi), :].astype(F32).T
        first = lax.broadcasted_iota(jnp.int32, qt.shape, 0) < DIFF_HEAD_DIM
        qs_ref[:, 0:tq] = jnp.where(first, qt, 0.0).astype(BF16)
        qs_ref[:, tq:] = jnp.where(first, 0.0, qt).astype(BF16)
        m_ref[...] = jnp.full(m_ref.shape, NEG_BIG, F32)
        acc_ref[...] = jnp.zeros(acc_ref.shape, F32)
        st0_ref[...] = scores(0, 0)

    def key_blocks(qi):
        def body(u, carry):
            group(ATTN_UNROLL * u, ATTN_UNROLL)
            return carry

        lax.fori_loop(0, qi // ATTN_UNROLL, body, 0)
        done = (qi // ATTN_UNROLL) * ATTN_UNROLL
        size = ATTN_UNROLL // 2
        while size >= ATTN_TAIL:
            take = (qi - done) >= size

            @pl.when(take)
            def _(done=done, size=size):
                group(done, size)

            done = done + jnp.where(take, size, 0)
            size //= 2
        for r in range(ATTN_TAIL):
            @pl.when(qi - done == r)
            def _(r=r, done=done):
                group(done, r)
                block(qi, True)

    def finish_tile(qi):
        o1 = acc_ref[0:HEAD_WIDTH, 0:tq] / acc_ref[HEAD_WIDTH:HEAD_WIDTH + 1, 0:tq]
        o2 = acc_ref[0:HEAD_WIDTH, tq:] / acc_ref[HEAD_WIDTH:HEAD_WIDTH + 1, tq:]
        ot = o1 - lam * o2
        inv = lax.rsqrt(jnp.mean(ot * ot, axis=0, keepdims=True) + NORM_EPS)
        yt = ot * inv * gsub_ref[...] * (1.0 - LAM_INIT)
        rows = rows_of(qi)
        o_ref[0, 0, rows, :] = (yt.T * ga_ref[0, 0, rows, :].astype(F32)).astype(BF16)

    start_tile(0)
    block(0, True)

    def tile(qi, carry):
        finish_tile(qi - 1)
        start_tile(qi)
        key_blocks(qi)
        return carry

    lax.fori_loop(1, nq, tile, 0)
    finish_tile(nq - 1)


def _diff_attn(q, k, vt, ga, g_subln_col, lq1, lk1, lq2, lk2):
    B, _, S, _ = q.shape
    tq = ATTN_TQ
    assert tq == ATTN_TK and S % tq == 0
    seq = pl.BlockSpec((1, 1, S, HEAD_WIDTH), lambda b, h: (b, h, 0, 0))
    vtblk = pl.BlockSpec((1, 1, HEAD_WIDTH, S), lambda b, h: (b, h, 0, 0))
    vec = lambda n: pl.BlockSpec((1, n), lambda b, h: (0, 0))
    return pl.pallas_call(
        _diff_attn_kernel,
        grid=(B, DIFF_HEADS),
        in_specs=[seq, seq, vtblk, seq,
                  pl.BlockSpec((HEAD_WIDTH, 1), lambda b, h: (0, 0)),
                  vec(DIFF_HEAD_DIM), vec(DIFF_HEAD_DIM), vec(DIFF_HEAD_DIM), vec(DIFF_HEAD_DIM)],
        out_specs=seq,
        out_shape=jax.ShapeDtypeStruct((B, DIFF_HEADS, S, HEAD_WIDTH), BF16),
        scratch_shapes=[
            pltpu.VMEM((HEAD_WIDTH, 2 * tq), BF16),
            pltpu.VMEM((S // ATTN_TK, VT_ROWS, ATTN_TK), BF16),
            pltpu.VMEM((1, 2 * tq), F32),
            pltpu.VMEM((VT_ROWS, 2 * tq), F32),
            pltpu.VMEM((ATTN_TK, tq), F32),
            pltpu.VMEM((ATTN_TK, tq), F32),
        ],
        compiler_params=pltpu.CompilerParams(
            dimension_semantics=("arbitrary", "arbitrary"),
            vmem_limit_bytes=VMEM_LIMIT),
        name="diff_attn",
    )(q, k, vt, ga, g_subln_col, lq1, lk1, lq2, lk2)


def _post_kernel(x_ref, mp_ref, ya_ref, wout_ref, gc_ref, wcq_ref, kv_ref, wco_ref,
                 gm_ref, wup_ref, wdn_ref, gf_ref, o_ref, oc_ref):
    x1 = (x_ref[0]
          + jnp.dot(mp_ref[0], wout_ref[0:POOL_WIDTH, :], preferred_element_type=F32)
          + jnp.dot(jnp.concatenate([ya_ref[0, hd] for hd in range(DIFF_HEADS)], axis=1),
                    wout_ref[POOL_WIDTH:, :], preferred_element_type=F32))
    hc = _rms(x1, gc_ref[...]).astype(BF16)
    qc = (jnp.dot(hc, wcq_ref[...], preferred_element_type=F32)
          * CROSS_HEAD_DIM ** -0.5).astype(BF16)
    for h in range(CROSS_HEADS):
        sl = slice(h * CROSS_HEAD_DIM, (h + 1) * CROSS_HEAD_DIM)
        kh = kv_ref[0, :, sl]
        vh = kv_ref[0, :, D_MODEL + h * CROSS_HEAD_DIM:D_MODEL + (h + 1) * CROSS_HEAD_DIM]
        s = lax.dot_general(qc[:, sl], kh, (((1,), (1,)), ((), ())),
                            preferred_element_type=F32)
        p = jnp.exp(s - jnp.max(s, axis=1, keepdims=True))
        l = jnp.sum(p, axis=1, keepdims=True)
        oh = jnp.dot(p.astype(BF16), vh, preferred_element_type=F32) / l
        oc_ref[:, sl] = oh.astype(BF16)
    x2 = x1 + jnp.dot(oc_ref[...], wco_ref[...], preferred_element_type=F32)

    hm = _rms(x2, gm_ref[...]).astype(BF16)
    acc = x2
    for c in range(D_FF // FF_CHUNK):
        sl = slice(c * FF_CHUNK, (c + 1) * FF_CHUNK)
        up = jnp.dot(hm, wup_ref[:, sl], preferred_element_type=F32)
        a = jnp.square(jnp.maximum(up, 0.0)).astype(BF16)
        acc = acc + jnp.dot(a, wdn_ref[sl, :], preferred_element_type=F32)
    o_ref[0] = _rms(acc, gf_ref[...])


def _post(x, mp, ya, w_out, g_cross, w_cq, kv, w_co, g_mlp, w_up, w_down, g_final):
    B, S, D = x.shape
    ts = ROW_TILE
    M = kv.shape[1]
    row = lambda b, s: (b, s, 0)
    const2 = lambda b, s: (0, 0)
    resident = lambda shape: pl.BlockSpec(shape, const2, pipeline_mode=pl.Buffered(1))
    return pl.pallas_call(
        _post_kernel,
        grid=(B, S // ts),
        in_specs=[
            pl.BlockSpec((1, ts, D), row),
            pl.BlockSpec((1, ts, POOL_WIDTH), row),
            pl.BlockSpec((1, DIFF_HEADS, ts, HEAD_WIDTH), lambda b, s: (b, 0, s, 0)),
            resident((D, D)),
            resident((1, D)),
            resident((D, D)),
            pl.BlockSpec((1, M, 2 * D), lambda b, s: (b, 0, 0)),
            resident((D, D)),
            resident((1, D)),
            resident((D, D_FF)),
            resident((D_FF, D)),
            resident((1, D)),
        ],
        out_specs=pl.BlockSpec((1, ts, D), row),
        out_shape=jax.ShapeDtypeStruct((B, S, D), F32),
        scratch_shapes=[pltpu.VMEM((ts, D), BF16)],
        compiler_params=pltpu.CompilerParams(
            dimension_semantics=("arbitrary", "arbitrary"),
            vmem_limit_bytes=VMEM_LIMIT),
        name="post",
    )(x, mp, ya, w_out, g_cross, w_cq, kv, w_co, g_mlp, w_up, w_down, g_final)


def _rope_tables(S):
    dh = DIFF_HEAD_DIM
    inv_freq = ROPE_THETA ** (-jnp.arange(0, dh, 2, dtype=F32) / dh)
    inv_freq = jnp.tile(inv_freq, LANES // (dh // 2))
    sign = jnp.where((jnp.arange(LANES) % dh) < dh // 2, -1.0, 1.0).astype(F32)
    assert S % LANES == 0
    ang_hi = (jnp.arange(S // LANES, dtype=F32) * LANES)[:, None] * inv_freq[None, :]
    ang_lo = jnp.arange(LANES, dtype=F32)[:, None] * inv_freq[None, :]
    ch, sh, cl, sl = lax.optimization_barrier(
        (jnp.cos(ang_hi), jnp.sin(ang_hi), jnp.cos(ang_lo), jnp.sin(ang_lo)))
    ch, sh, cl, sl = ch[:, None, :], sh[:, None, :], cl[None, :, :], sl[None, :, :]
    cos_t = (ch * cl - sh * sl).reshape(S, LANES)
    sin_t = ((sh * cl + ch * sl) * sign).reshape(S, LANES)
    return cos_t, sin_t


def kernel(x, mem, g_mix, w_in, w_pool, pool_scale, lambda_q1, lambda_k1, lambda_q2,
           lambda_k2, g_subln, w_out, g_cross, g_mem, w_cq, w_ckv, w_co, g_mlp, w_up,
           w_down, g_final):
    B, S, _ = x.shape
    assert g_mix.shape[0] == 1, "single-layer block"
    cos_t, sin_t = _rope_tables(S)
    mp, q, k, vt, ga = _in_proj(x, g_mix, w_in[0].astype(BF16), w_pool[0].astype(BF16),
                                pool_scale, cos_t, sin_t)
    kv = _mem_kv(mem, g_mem, w_ckv[0].astype(BF16))
    ya = _diff_attn(q, k, vt, ga, g_subln.reshape(HEAD_WIDTH, 1), lambda_q1, lambda_k1,
                    lambda_q2, lambda_k2)
    return _post(x, mp, ya, w_out[0].astype(BF16), g_cross, w_cq[0].astype(BF16), kv,
                 w_co[0].astype(BF16), g_mlp, w_up[0].astype(BF16), w_down[0].astype(BF16),
                 g_final.reshape(1, D_MODEL))
```

```python
import math

import jax
import jax.numpy as jnp
from jax import lax
from jax.experimental import pallas as pl
from jax.experimental.pallas import tpu as pltpu

D_MODEL = 1024
POOL_WIDTH = 512
POOL_WINDOWS = (2, 4, 8, 16)
POOL_GROUP_WIDTH = 128
POOL_HALO = 16
DIFF_HEADS = 4
DIFF_HEAD_DIM = 64
HEAD_WIDTH = 2 * DIFF_HEAD_DIM
Q_WIDTH = DIFF_HEADS * HEAD_WIDTH
Q_LO = POOL_WIDTH
K_LO = Q_LO + Q_WIDTH
V_LO = K_LO + Q_WIDTH
GATE_LO = V_LO + Q_WIDTH
IN_PROJ_WIDTH = GATE_LO + POOL_WIDTH + Q_WIDTH
ROPE_THETA = 10000.0
CROSS_HEADS = 4
CROSS_HEAD_DIM = 256
D_FF = 4096
NORM_EPS = 1e-6
NEG_BIG = -1e30
LAM_INIT = 0.8 - 0.6 * math.exp(-0.3 * 0)
LOG2E = math.log2(math.e)

LANES = 128
BF16_SUBLANES = 16
VMEM_LIMIT = 60 * 1024 * 1024

PROJ_TILE = 1024
ATTN_TQ = 512
ATTN_TK = 512
ATTN_UNROLL = 16
ATTN_TAIL = 4
ROW_TILE = 1024
FF_CHUNK = 1024
VT_ROWS = HEAD_WIDTH + BF16_SUBLANES

BF16 = jnp.bfloat16
F32 = jnp.float32


def _rms(x, g):
    return x * lax.rsqrt(jnp.mean(x * x, axis=-1, keepdims=True) + NORM_EPS) * g


def _rotate_half_pairs(x):
    lane = lax.broadcasted_iota(jnp.int32, x.shape, 1)
    half = DIFF_HEAD_DIM // 2
    lower = (lane % DIFF_HEAD_DIM) < half
    return jnp.where(lower, pltpu.roll(x, LANES - half, axis=1), pltpu.roll(x, half, axis=1))


def _in_proj_kernel(x_ref, g_ref, w_ref, wpool_ref, pscale_ref, cos_ref, sin_ref,
                    mp_ref, q_ref, k_ref, vt_ref, ga_ref, ext_ref):
    si = pl.program_id(1)
    ts = x_ref.shape[1]
    h = _rms(x_ref[0], g_ref[...]).astype(BF16)

    def proj(lo, hi):
        return jnp.dot(h, w_ref[:, lo:hi], preferred_element_type=F32)

    gates = 0.5 + 0.5 * jnp.tanh(0.5 * proj(GATE_LO, IN_PROJ_WIDTH))
    for hd in range(DIFF_HEADS):
        lo = POOL_WIDTH + hd * HEAD_WIDTH
        ga_ref[0, hd] = gates[:, lo:lo + HEAD_WIDTH].astype(BF16)

    @pl.when(si == 0)
    def _():
        ext_ref[0:POOL_HALO, :] = jnp.zeros((POOL_HALO, POOL_WIDTH), F32)

    ext_ref[POOL_HALO:, :] = proj(0, POOL_WIDTH)
    q = proj(Q_LO, K_LO)
    k = proj(K_LO, V_LO)
    v = proj(V_LO, GATE_LO)

    pos = si * ts + lax.broadcasted_iota(jnp.int32, (ts, 1), 0)
    zs = []
    for g, w in enumerate(POOL_WINDOWS):
        lo, hi = g * POOL_GROUP_WIDTH, (g + 1) * POOL_GROUP_WIDTH
        e = ext_ref[:, lo:hi]
        win = e
        shift = 1
        while shift < w:
            win = win + pltpu.roll(win, shift, axis=0)
            shift *= 2
        cnt = jnp.minimum(pos + 1, w).astype(F32)
        zs.append((win[POOL_HALO:] / cnt - e[POOL_HALO:]).astype(BF16))
    ext_ref[0:POOL_HALO, :] = ext_ref[ts:ts + POOL_HALO, :]

    cos = cos_ref[...]
    sin = sin_ref[...]
    qscale = DIFF_HEAD_DIM ** -0.5 * LOG2E
    for c in range(Q_WIDTH // LANES):
        sl = slice(c * LANES, (c + 1) * LANES)
        qc = q[:, sl]
        kc = k[:, sl]
        q_ref[0, c] = ((qc * cos + _rotate_half_pairs(qc) * sin) * qscale).astype(BF16)
        k_ref[0, c] = (kc * cos + _rotate_half_pairs(kc) * sin).astype(BF16)

    for hd in range(DIFF_HEADS):
        vt_ref[0, hd] = v[:, hd * HEAD_WIDTH:(hd + 1) * HEAD_WIDTH].T.astype(BF16)

    for g in range(len(POOL_WINDOWS)):
        lo, hi = g * POOL_GROUP_WIDTH, (g + 1) * POOL_GROUP_WIDTH
        y = jnp.dot(zs[g], wpool_ref[g], preferred_element_type=F32)
        y = y * pscale_ref[:, lo:hi] * gates[:, lo:hi]
        mp_ref[0, :, lo:hi] = y.astype(BF16)


def _in_proj(x, g_mix, w_in, w_pool, pool_scale, cos_t, sin_t):
    B, S, D = x.shape
    ts = PROJ_TILE
    row = lambda b, s: (b, s, 0)
    const2 = lambda b, s: (0, 0)
    half = pl.BlockSpec((1, ts, POOL_WIDTH), row)
    half_sds = jax.ShapeDtypeStruct((B, S, POOL_WIDTH), BF16)
    heads = pl.BlockSpec((1, DIFF_HEADS, ts, HEAD_WIDTH), lambda b, s: (b, 0, s, 0))
    heads_sds = jax.ShapeDtypeStruct((B, DIFF_HEADS, S, HEAD_WIDTH), BF16)
    vt_spec = pl.BlockSpec((1, DIFF_HEADS, HEAD_WIDTH, ts), lambda b, s: (b, 0, 0, s))
    vt_sds = jax.ShapeDtypeStruct((B, DIFF_HEADS, HEAD_WIDTH, S), BF16)
    return pl.pallas_call(
        _in_proj_kernel,
        grid=(B, S // ts),
        in_specs=[
            pl.BlockSpec((1, ts, D), row),
            pl.BlockSpec((1, D), const2),
            pl.BlockSpec((D, IN_PROJ_WIDTH), const2),
            pl.BlockSpec((len(POOL_WINDOWS), POOL_GROUP_WIDTH, POOL_GROUP_WIDTH),
                         lambda b, s: (0, 0, 0)),
            pl.BlockSpec((1, POOL_WIDTH), const2),
            pl.BlockSpec((ts, LANES), lambda b, s: (s, 0)),
            pl.BlockSpec((ts, LANES), lambda b, s: (s, 0)),
        ],
        out_specs=[half, heads, heads, vt_spec, heads],
        out_shape=[half_sds, heads_sds, heads_sds, vt_sds, heads_sds],
        scratch_shapes=[pltpu.VMEM((ts + POOL_HALO, POOL_WIDTH), F32)],
        compiler_params=pltpu.CompilerParams(
            dimension_semantics=("arbitrary", "arbitrary"),
            vmem_limit_bytes=VMEM_LIMIT),
        name="in_proj",
    )(x, g_mix, w_in, w_pool, pool_scale, cos_t, sin_t)


def _mem_kv_kernel(mem_ref, g_ref, w_ref, kv_ref):
    mn = _rms(mem_ref[0], g_ref[...]).astype(BF16)
    kv_ref[0] = jnp.dot(mn, w_ref[...], preferred_element_type=F32).astype(BF16)


def _mem_kv(mem, g_mem, w_ckv):
    B, M, D = mem.shape
    return pl.pallas_call(
        _mem_kv_kernel,
        grid=(B,),
        in_specs=[
            pl.BlockSpec((1, M, D), lambda b: (b, 0, 0)),
            pl.BlockSpec((1, D), lambda b: (0, 0)),
            pl.BlockSpec((D, 2 * D), lambda b: (0, 0)),
        ],
        out_specs=pl.BlockSpec((1, M, 2 * D), lambda b: (b, 0, 0)),
        out_shape=jax.ShapeDtypeStruct((B, M, 2 * D), BF16),
        compiler_params=pltpu.CompilerParams(
            dimension_semantics=("arbitrary",), vmem_limit_bytes=VMEM_LIMIT),
        name="mem_kv",
    )(mem, g_mem, w_ckv)


def _diff_attn_kernel(q_ref, k_ref, vt_ref, ga_ref, gsub_ref, lq1_ref, lk1_ref,
                      lq2_ref, lk2_ref, o_ref, qs_ref, vx_ref, m_ref, acc_ref, st0_ref, st1_ref):
    tq = ATTN_TQ
    tk = ATTN_TK
    nkb = vx_ref.shape[0]
    nq = q_ref.shape[2] // tq

    for j in range(nkb):
        vx_ref[j, 0:HEAD_WIDTH, :] = vt_ref[0, 0, :, j * tk:(j + 1) * tk]
        vx_ref[j, HEAD_WIDTH:, :] = jnp.ones((BF16_SUBLANES, tk), BF16)

    lam = (jnp.exp(jnp.sum(lq1_ref[...] * lk1_ref[...], keepdims=True))
           - jnp.exp(jnp.sum(lq2_ref[...] * lk2_ref[...], keepdims=True)) + LAM_INIT)

    def scores(kb, c):
        start = pl.multiple_of(kb * tk, tk)
        return jnp.dot(k_ref[0, 0, pl.ds(start, tk), :], qs_ref[:, c * tq:(c + 1) * tq],
                       preferred_element_type=F32)

    def softmax(st, c, masked):
        cols = slice(c * tq, (c + 1) * tq)
        if masked:
            key = lax.broadcasted_iota(jnp.int32, st.shape, 0)
            qry = lax.broadcasted_iota(jnp.int32, st.shape, 1)
            st = jnp.where(key <= qry, st, NEG_BIG)
        m_prev = m_ref[:, cols]
        m_new = jnp.maximum(m_prev, jnp.max(st, axis=0, keepdims=True))
        m_ref[:, cols] = m_new
        return jnp.exp2(m_prev - m_new), jnp.exp2(st - m_new).astype(BF16)

    def pv(kb, c, alpha, pt):
        cols = slice(c * tq, (c + 1) * tq)
        acc_ref[:, cols] = alpha * acc_ref[:, cols] + jnp.dot(
            vx_ref[kb], pt, preferred_element_type=F32)

    def block(kb, masked):
        st1_ref[...] = scores(kb, 1)
        a0, p0 = softmax(st0_ref[...], 0, masked)
        pv(kb, 0, a0, p0)
        if not masked:
            st0_ref[...] = scores(kb + 1, 0)
        a1, p1 = softmax(st1_ref[...], 1, masked)
        pv(kb, 1, a1, p1)

    def group(base, n):
        for j in range(n):
            block(base + j, False)

    def rows_of(qi):
        return pl.ds(pl.multiple_of(qi * tq, tq), tq)

    def start_tile(qi):
        qt = q_ref[0, 0, rows_of(qi), :].astype(F32).T
        first = lax.broadcasted_iota(jnp.int32, qt.shape, 0) < DIFF_HEAD_DIM
        qs_ref[:, 0:tq] = jnp.where(first, qt, 0.0).astype(BF16)
        qs_ref[:, tq:] = jnp.where(first, 0.0, qt).astype(BF16)
        m_ref[...] = jnp.full(m_ref.shape, NEG_BIG, F32)
        acc_ref[...] = jnp.zeros(acc_ref.shape, F32)
        st0_ref[...] = scores(0, 0)

    def key_blocks(qi):
        def body(u, carry):
            group(ATTN_UNROLL * u, ATTN_UNROLL)
            return carry

        lax.fori_loop(0, qi // ATTN_UNROLL, body, 0)
        done = (qi // ATTN_UNROLL) * ATTN_UNROLL
        size = ATTN_UNROLL // 2
        while size >= ATTN_TAIL:
            take = (qi - done) >= size

            @pl.when(take)
            def _(done=done, size=size):
                group(done, size)

            done = done + jnp.where(take, size, 0)
            size //= 2
        for r in range(ATTN_TAIL):
            @pl.when(qi - done == r)
            def _(r=r, done=done):
                group(done, r)
                block(qi, True)
                finish_tile(qi)
                start_tile(jnp.minimum(qi + 1, nq - 1))

    def finish_tile(qi):
        o1 = acc_ref[0:HEAD_WIDTH, 0:tq] / acc_ref[HEAD_WIDTH:HEAD_WIDTH + 1, 0:tq]
        o2 = acc_ref[0:HEAD_WIDTH, tq:] / acc_ref[HEAD_WIDTH:HEAD_WIDTH + 1, tq:]
        ot = o1 - lam * o2
        inv = lax.rsqrt(jnp.mean(ot * ot, axis=0, keepdims=True) + NORM_EPS)
        yt = ot * inv * gsub_ref[...] * (1.0 - LAM_INIT)
        rows = rows_of(qi)
        o_ref[0, 0, rows, :] = (yt.T * ga_ref[0, 0, rows, :].astype(F32)).astype(BF16)

    start_tile(0)

    def tile(qi, carry):
        key_blocks(qi)
        return carry

    lax.fori_loop(0, nq, tile, 0)


def _diff_attn(q, k, vt, ga, g_subln_col, lq1, lk1, lq2, lk2):
    B, _, S, _ = q.shape
    tq = ATTN_TQ
    assert tq == ATTN_TK and S % tq == 0
    seq = pl.BlockSpec((1, 1, S, HEAD_WIDTH), lambda b, h: (b, h, 0, 0))
    vtblk = pl.BlockSpec((1, 1, HEAD_WIDTH, S), lambda b, h: (b, h, 0, 0))
    vec = lambda n: pl.BlockSpec((1, n), lambda b, h: (0, 0))
    return pl.pallas_call(
        _diff_attn_kernel,
        grid=(B, DIFF_HEADS),
        in_specs=[seq, seq, vtblk, seq,
                  pl.BlockSpec((HEAD_WIDTH, 1), lambda b, h: (0, 0)),
                  vec(DIFF_HEAD_DIM), vec(DIFF_HEAD_DIM), vec(DIFF_HEAD_DIM), vec(DIFF_HEAD_DIM)],
        out_specs=seq,
        out_shape=jax.ShapeDtypeStruct((B, DIFF_HEADS, S, HEAD_WIDTH), BF16),
        scratch_shapes=[
            pltpu.VMEM((HEAD_WIDTH, 2 * tq), BF16),
            pltpu.VMEM((S // ATTN_TK, VT_ROWS, ATTN_TK), BF16),
            pltpu.VMEM((1, 2 * tq), F32),
            pltpu.VMEM((VT_ROWS, 2 * tq), F32),
            pltpu.VMEM((ATTN_TK, tq), F32),
            pltpu.VMEM((ATTN_TK, tq), F32),
        ],
        compiler_params=pltpu.CompilerParams(
            dimension_semantics=("arbitrary", "arbitrary"),
            vmem_limit_bytes=VMEM_LIMIT),
        name="diff_attn",
    )(q, k, vt, ga, g_subln_col, lq1, lk1, lq2, lk2)


def _post_kernel(x_ref, mp_ref, ya_ref, wout_ref, gc_ref, wcq_ref, kv_ref, wco_ref,
                 gm_ref, wup_ref, wdn_ref, gf_ref, o_ref, oc_ref):
    x1 = (x_ref[0]
          + jnp.dot(mp_ref[0], wout_ref[0:POOL_WIDTH, :], preferred_element_type=F32)
          + jnp.dot(jnp.concatenate([ya_ref[0, hd] for hd in range(DIFF_HEADS)], axis=1),
                    wout_ref[POOL_WIDTH:, :], preferred_element_type=F32))
    hc = _rms(x1, gc_ref[...]).astype(BF16)
    qc = (jnp.dot(hc, wcq_ref[...], preferred_element_type=F32)
          * CROSS_HEAD_DIM ** -0.5).astype(BF16)
    for h in range(CROSS_HEADS):
        sl = slice(h * CROSS_HEAD_DIM, (h + 1) * CROSS_HEAD_DIM)
        kh = kv_ref[0, :, sl]
        vh = kv_ref[0, :, D_MODEL + h * CROSS_HEAD_DIM:D_MODEL + (h + 1) * CROSS_HEAD_DIM]
        s = lax.dot_general(qc[:, sl], kh, (((1,), (1,)), ((), ())),
                            preferred_element_type=F32)
        p = jnp.exp(s - jnp.max(s, axis=1, keepdims=True))
        l = jnp.sum(p, axis=1, keepdims=True)
        oh = jnp.dot(p.astype(BF16), vh, preferred_element_type=F32) / l
        oc_ref[:, sl] = oh.astype(BF16)
    x2 = x1 + jnp.dot(oc_ref[...], wco_ref[...], preferred_element_type=F32)

    hm = _rms(x2, gm_ref[...]).astype(BF16)
    acc = x2
    for c in range(D_FF // FF_CHUNK):
        sl = slice(c * FF_CHUNK, (c + 1) * FF_CHUNK)
        up = jnp.dot(hm, wup_ref[:, sl], preferred_element_type=F32)
        a = jnp.square(jnp.maximum(up, 0.0)).astype(BF16)
        acc = acc + jnp.dot(a, wdn_ref[sl, :], preferred_element_type=F32)
    o_ref[0] = _rms(acc, gf_ref[...])


def _post(x, mp, ya, w_out, g_cross, w_cq, kv, w_co, g_mlp, w_up, w_down, g_final):
    B, S, D = x.shape
    ts = ROW_TILE
    M = kv.shape[1]
    row = lambda b, s: (b, s, 0)
    const2 = lambda b, s: (0, 0)
    resident = lambda shape: pl.BlockSpec(shape, const2, pipeline_mode=pl.Buffered(1))
    return pl.pallas_call(
        _post_kernel,
        grid=(B, S // ts),
        in_specs=[
            pl.BlockSpec((1, ts, D), row),
            pl.BlockSpec((1, ts, POOL_WIDTH), row),
            pl.BlockSpec((1, DIFF_HEADS, ts, HEAD_WIDTH), lambda b, s: (b, 0, s, 0)),
            resident((D, D)),
            resident((1, D)),
            resident((D, D)),
            pl.BlockSpec((1, M, 2 * D), lambda b, s: (b, 0, 0)),
            resident((D, D)),
            resident((1, D)),
            resident((D, D_FF)),
            resident((D_FF, D)),
            resident((1, D)),
        ],
        out_specs=pl.BlockSpec((1, ts, D), row),
        out_shape=jax.ShapeDtypeStruct((B, S, D), F32),
        scratch_shapes=[pltpu.VMEM((ts, D), BF16)],
        compiler_params=pltpu.CompilerParams(
            dimension_semantics=("arbitrary", "arbitrary"),
            vmem_limit_bytes=VMEM_LIMIT),
        name="post",
    )(x, mp, ya, w_out, g_cross, w_cq, kv, w_co, g_mlp, w_up, w_down, g_final)


def _rope_tables(S):
    dh = DIFF_HEAD_DIM
    inv_freq = ROPE_THETA ** (-jnp.arange(0, dh, 2, dtype=F32) / dh)
    inv_freq = jnp.tile(inv_freq, LANES // (dh // 2))
    sign = jnp.where((jnp.arange(LANES) % dh) < dh // 2, -1.0, 1.0).astype(F32)
    assert S % LANES == 0
    ang_hi = (jnp.arange(S // LANES, dtype=F32) * LANES)[:, None] * inv_freq[None, :]
    ang_lo = jnp.arange(LANES, dtype=F32)[:, None] * inv_freq[None, :]
    ch, sh, cl, sl = lax.optimization_barrier(
        (jnp.cos(ang_hi), jnp.sin(ang_hi), jnp.cos(ang_lo), jnp.sin(ang_lo)))
    ch, sh, cl, sl = ch[:, None, :], sh[:, None, :], cl[None, :, :], sl[None, :, :]
    cos_t = (ch * cl - sh * sl).reshape(S, LANES)
    sin_t = ((sh * cl + ch * sl) * sign).reshape(S, LANES)
    return cos_t, sin_t


def kernel(x, mem, g_mix, w_in, w_pool, pool_scale, lambda_q1, lambda_k1, lambda_q2,
           lambda_k2, g_subln, w_out, g_cross, g_mem, w_cq, w_ckv, w_co, g_mlp, w_up,
           w_down, g_final):
    B, S, _ = x.shape
    assert g_mix.shape[0] == 1, "single-layer block"
    cos_t, sin_t = _rope_tables(S)
    mp, q, k, vt, ga = _in_proj(x, g_mix, w_in[0].astype(BF16), w_pool[0].astype(BF16),
                                pool_scale, cos_t, sin_t)
    kv = _mem_kv(mem, g_mem, w_ckv[0].astype(BF16))
    ya = _diff_attn(q, k, vt, ga, g_subln.reshape(HEAD_WIDTH, 1), lambda_q1, lambda_k1,
                    lambda_q2, lambda_k2)
    return _post(x, mp, ya, w_out[0].astype(BF16), g_cross, w_cq[0].astype(BF16), kv,
                 w_co[0].astype(BF16), g_mlp, w_up[0].astype(BF16), w_down[0].astype(BF16),
                 g_final.reshape(1, D_MODEL))
```

```python
import math

import jax
import jax.numpy as jnp
from jax import lax
from jax.experimental import pallas as pl
from jax.experimental.pallas import tpu as pltpu

D_MODEL = 1024
POOL_WIDTH = 512
POOL_WINDOWS = (2, 4, 8, 16)
POOL_GROUP_WIDTH = 128
POOL_HALO = 16
DIFF_HEADS = 4
DIFF_HEAD_DIM = 64
HEAD_WIDTH = 2 * DIFF_HEAD_DIM
Q_WIDTH = DIFF_HEADS * HEAD_WIDTH
Q_LO = POOL_WIDTH
K_LO = Q_LO + Q_WIDTH
V_LO = K_LO + Q_WIDTH
GATE_LO = V_LO + Q_WIDTH
IN_PROJ_WIDTH = GATE_LO + POOL_WIDTH + Q_WIDTH
ROPE_THETA = 10000.0
CROSS_HEADS = 4
CROSS_HEAD_DIM = 256
D_FF = 4096
NORM_EPS = 1e-6
NEG_BIG = -1e30
LAM_INIT = 0.8 - 0.6 * math.exp(-0.3 * 0)
LOG2E = math.log2(math.e)

LANES = 128
BF16_SUBLANES = 16
VMEM_LIMIT = 60 * 1024 * 1024

PROJ_TILE = 1024
ATTN_TQ = 512
ATTN_TK = 512
ATTN_UNROLL = 16
ATTN_TAIL = 4
ROW_TILE = 1024
FF_CHUNK = 1024
VT_ROWS = HEAD_WIDTH + BF16_SUBLANES

BF16 = jnp.bfloat16
F32 = jnp.float32


def _rms(x, g):
    return x * lax.rsqrt(jnp.mean(x * x, axis=-1, keepdims=True) + NORM_EPS) * g


def _rotate_half_pairs(x):
    lane = lax.broadcasted_iota(jnp.int32, x.shape, 1)
    half = DIFF_HEAD_DIM // 2
    lower = (lane % DIFF_HEAD_DIM) < half
    return jnp.where(lower, pltpu.roll(x, LANES - half, axis=1), pltpu.roll(x, half, axis=1))


def _in_proj_kernel(x_ref, g_ref, w_ref, wpool_ref, pscale_ref, cos_ref, sin_ref,
                    mp_ref, q_ref, k_ref, vt_ref, ga_ref, ext_ref):
    si = pl.program_id(1)
    ts = x_ref.shape[1]
    h = _rms(x_ref[0], g_ref[...]).astype(BF16)

    def proj(lo, hi):
        return jnp.dot(h, w_ref[:, lo:hi], preferred_element_type=F32)

    gates = 0.5 + 0.5 * jnp.tanh(0.5 * proj(GATE_LO, IN_PROJ_WIDTH))
    for hd in range(DIFF_HEADS):
        lo = POOL_WIDTH + hd * HEAD_WIDTH
        ga_ref[0, hd] = gates[:, lo:lo + HEAD_WIDTH].astype(BF16)

    @pl.when(si == 0)
    def _():
        ext_ref[0:POOL_HALO, :] = jnp.zeros((POOL_HALO, POOL_WIDTH), F32)

    ext_ref[POOL_HALO:, :] = proj(0, POOL_WIDTH)
    q = proj(Q_LO, K_LO)
    k = proj(K_LO, V_LO)
    v = proj(V_LO, GATE_LO)

    pos = si * ts + lax.broadcasted_iota(jnp.int32, (ts, 1), 0)
    zs = []
    for g, w in enumerate(POOL_WINDOWS):
        lo, hi = g * POOL_GROUP_WIDTH, (g + 1) * POOL_GROUP_WIDTH
        e = ext_ref[:, lo:hi]
        win = e
        shift = 1
        while shift < w:
            win = win + pltpu.roll(win, shift, axis=0)
            shift *= 2
        cnt = jnp.minimum(pos + 1, w).astype(F32)
        zs.append((win[POOL_HALO:] / cnt - e[POOL_HALO:]).astype(BF16))
    ext_ref[0:POOL_HALO, :] = ext_ref[ts:ts + POOL_HALO, :]

    cos = cos_ref[...]
    sin = sin_ref[...]
    qscale = DIFF_HEAD_DIM ** -0.5 * LOG2E
    for c in range(Q_WIDTH // LANES):
        sl = slice(c * LANES, (c + 1) * LANES)
        qc = q[:, sl]
        kc = k[:, sl]
        q_ref[0, c] = ((qc * cos + _rotate_half_pairs(qc) * sin) * qscale).astype(BF16)
        k_ref[0, c] = (kc * cos + _rotate_half_pairs(kc) * sin).astype(BF16)

    for hd in range(DIFF_HEADS):
        vt_ref[0, hd] = v[:, hd * HEAD_WIDTH:(hd + 1) * HEAD_WIDTH].T.astype(BF16)

    for g in range(len(POOL_WINDOWS)):
        lo, hi = g * POOL_GROUP_WIDTH, (g + 1) * POOL_GROUP_WIDTH
        y = jnp.dot(zs[g], wpool_ref[g], preferred_element_type=F32)
        y = y * pscale_ref[:, lo:hi] * gates[:, lo:hi]
        mp_ref[0, :, lo:hi] = y.astype(BF16)


def _in_proj(x, g_mix, w_in, w_pool, pool_scale, cos_t, sin_t):
    B, S, D = x.shape
    ts = PROJ_TILE
    row = lambda b, s: (b, s, 0)
    const2 = lambda b, s: (0, 0)
    half = pl.BlockSpec((1, ts, POOL_WIDTH), row)
    half_sds = jax.ShapeDtypeStruct((B, S, POOL_WIDTH), BF16)
    heads = pl.BlockSpec((1, DIFF_HEADS, ts, HEAD_WIDTH), lambda b, s: (b, 0, s, 0))
    heads_sds = jax.ShapeDtypeStruct((B, DIFF_HEADS, S, HEAD_WIDTH), BF16)
    vt_spec = pl.BlockSpec((1, DIFF_HEADS, HEAD_WIDTH, ts), lambda b, s: (b, 0, 0, s))
    vt_sds = jax.ShapeDtypeStruct((B, DIFF_HEADS, HEAD_WIDTH, S), BF16)
    return pl.pallas_call(
        _in_proj_kernel,
        grid=(B, S // ts),
        in_specs=[
            pl.BlockSpec((1, ts, D), row),
            pl.BlockSpec((1, D), const2),
            pl.BlockSpec((D, IN_PROJ_WIDTH), const2),
            pl.BlockSpec((len(POOL_WINDOWS), POOL_GROUP_WIDTH, POOL_GROUP_WIDTH),
                         lambda b, s: (0, 0, 0)),
            pl.BlockSpec((1, POOL_WIDTH), const2),
            pl.BlockSpec((ts, LANES), lambda b, s: (s, 0)),
            pl.BlockSpec((ts, LANES), lambda b, s: (s, 0)),
        ],
        out_specs=[half, heads, heads, vt_spec, heads],
        out_shape=[half_sds, heads_sds, heads_sds, vt_sds, heads_sds],
        scratch_shapes=[pltpu.VMEM((ts + POOL_HALO, POOL_WIDTH), F32)],
        compiler_params=pltpu.CompilerParams(
            dimension_semantics=("arbitrary", "arbitrary"),
            vmem_limit_bytes=VMEM_LIMIT),
        name="in_proj",
    )(x, g_mix, w_in, w_pool, pool_scale, cos_t, sin_t)


def _mem_kv_kernel(mem_ref, g_ref, w_ref, kv_ref):
    mn = _rms(mem_ref[0], g_ref[...]).astype(BF16)
    kv_ref[0] = jnp.dot(mn, w_ref[...], preferred_element_type=F32).astype(BF16)


def _mem_kv(mem, g_mem, w_ckv):
    B, M, D = mem.shape
    return pl.pallas_call(
        _mem_kv_kernel,
        grid=(B,),
        in_specs=[
            pl.BlockSpec((1, M, D), lambda b: (b, 0, 0)),
            pl.BlockSpec((1, D), lambda b: (0, 0)),
            pl.BlockSpec((D, 2 * D), lambda b: (0, 0)),
        ],
        out_specs=pl.BlockSpec((1, M, 2 * D), lambda b: (b, 0, 0)),
        out_shape=jax.ShapeDtypeStruct((B, M, 2 * D), BF16),
        compiler_params=pltpu.CompilerParams(
            dimension_semantics=("arbitrary",), vmem_limit_bytes=VMEM_LIMIT),
        name="mem_kv",
    )(mem, g_mem, w_ckv)


def _diff_attn_kernel(q_ref, k_ref, vt_ref, ga_ref, gsub_ref, lq1_ref, lk1_ref,
                      lq2_ref, lk2_ref, o_ref, qs_ref, vx_ref, m_ref, acc_ref, st_ref, mc_ref):
    tq = ATTN_TQ
    tk = ATTN_TK
    nkb = vx_ref.shape[0]
    nq = q_ref.shape[2] // tq

    for j in range(nkb):
        vx_ref[j, 0:HEAD_WIDTH, :] = vt_ref[0, 0, :, j * tk:(j + 1) * tk]
        vx_ref[j, HEAD_WIDTH:, :] = jnp.ones((BF16_SUBLANES, tk), BF16)

    lam = (jnp.exp(jnp.sum(lq1_ref[...] * lk1_ref[...], keepdims=True))
           - jnp.exp(jnp.sum(lq2_ref[...] * lk2_ref[...], keepdims=True)) + LAM_INIT)

    def scores(kb, c, slot):
        start = pl.multiple_of(kb * tk, tk)
        st = jnp.dot(k_ref[0, 0, pl.ds(start, tk), :], qs_ref[:, c * tq:(c + 1) * tq],
                     preferred_element_type=F32)
        st_ref[slot, c] = st
        mc_ref[slot, :, c * tq:(c + 1) * tq] = jnp.max(st, axis=0, keepdims=True)

    def softmax(slot, c, masked):
        cols = slice(c * tq, (c + 1) * tq)
        st = st_ref[slot, c]
        if masked:
            key = lax.broadcasted_iota(jnp.int32, st.shape, 0)
            qry = lax.broadcasted_iota(jnp.int32, st.shape, 1)
            st = jnp.where(key <= qry, st, NEG_BIG)
            m_cur = jnp.max(st, axis=0, keepdims=True)
        else:
            m_cur = mc_ref[slot, :, cols]
        m_prev = m_ref[:, cols]
        m_new = jnp.maximum(m_prev, m_cur)
        m_ref[:, cols] = m_new
        return jnp.exp2(m_prev - m_new), jnp.exp2(st - m_new).astype(BF16)

    def pv(kb, c, alpha, pt):
        cols = slice(c * tq, (c + 1) * tq)
        acc_ref[:, cols] = alpha * acc_ref[:, cols] + jnp.dot(
            vx_ref[kb], pt, preferred_element_type=F32)

    def block(kb, slot, masked):
        if not masked:
            scores(kb + 1, 0, 1 - slot)
            scores(kb + 1, 1, 1 - slot)
        a0, p0 = softmax(slot, 0, masked)
        pv(kb, 0, a0, p0)
        a1, p1 = softmax(slot, 1, masked)
        pv(kb, 1, a1, p1)

    def group(base, n):
        for j in range(n):
            block(base + j, j % 2, False)

    def rows_of(qi):
        return pl.ds(pl.multiple_of(qi * tq, tq), tq)

    def start_tile(qi):
        qt = q_ref[0, 0, rows_of(qi), :].astype(F32).T
        first = lax.broadcasted_iota(jnp.int32, qt.shape, 0) < DIFF_HEAD_DIM
        qs_ref[:, 0:tq] = jnp.where(first, qt, 0.0).astype(BF16)
        qs_ref[:, tq:] = jnp.where(first, 0.0, qt).astype(BF16)
        m_ref[...] = jnp.full(m_ref.shape, NEG_BIG, F32)
        acc_ref[...] = jnp.zeros(acc_ref.shape, F32)
        scores(0, 0, 0)
        scores(0, 1, 0)

    def key_blocks(qi):
        def body(u, carry):
            group(ATTN_UNROLL * u, ATTN_UNROLL)
            return carry

        lax.fori_loop(0, qi // ATTN_UNROLL, body, 0)
        done = (qi // ATTN_UNROLL) * ATTN_UNROLL
        size = ATTN_UNROLL // 2
        while size >= ATTN_TAIL:
            take = (qi - done) >= size

            @pl.when(take)
            def _(done=done, size=size):
                group(done, size)

            done = done + jnp.where(take, size, 0)
            size //= 2
        for r in range(ATTN_TAIL):
            @pl.when(qi - done == r)
            def _(r=r, done=done):
                group(done, r)
                block(qi, r % 2, True)
                finish_tile(qi)
                start_tile(jnp.minimum(qi + 1, nq - 1))

    def finish_tile(qi):
        o1 = acc_ref[0:HEAD_WIDTH, 0:tq] / acc_ref[HEAD_WIDTH:HEAD_WIDTH + 1, 0:tq]
        o2 = acc_ref[0:HEAD_WIDTH, tq:] / acc_ref[HEAD_WIDTH:HEAD_WIDTH + 1, tq:]
        ot = o1 - lam * o2
        inv = lax.rsqrt(jnp.mean(ot * ot, axis=0, keepdims=True) + NORM_EPS)
        yt = ot * inv * gsub_ref[...] * (1.0 - LAM_INIT)
        rows = rows_of(qi)
        o_ref[0, 0, rows, :] = (yt.T * ga_ref[0, 0, rows, :].astype(F32)).astype(BF16)

    start_tile(0)

    def tile(qi, carry):
        key_blocks(qi)
        return carry

    lax.fori_loop(0, nq, tile, 0)


def _diff_attn(q, k, vt, ga, g_subln_col, lq1, lk1, lq2, lk2):
    B, _, S, _ = q.shape
    tq = ATTN_TQ
    assert tq == ATTN_TK and S % tq == 0
    seq = pl.BlockSpec((1, 1, S, HEAD_WIDTH), lambda b, h: (b, h, 0, 0))
    vtblk = pl.BlockSpec((1, 1, HEAD_WIDTH, S), lambda b, h: (b, h, 0, 0))
    vec = lambda n: pl.BlockSpec((1, n), lambda b, h: (0, 0))
    return pl.pallas_call(
        _diff_attn_kernel,
        grid=(B, DIFF_HEADS),
        in_specs=[seq, seq, vtblk, seq,
                  pl.BlockSpec((HEAD_WIDTH, 1), lambda b, h: (0, 0)),
                  vec(DIFF_HEAD_DIM), vec(DIFF_HEAD_DIM), vec(DIFF_HEAD_DIM), vec(DIFF_HEAD_DIM)],
        out_specs=seq,
        out_shape=jax.ShapeDtypeStruct((B, DIFF_HEADS, S, HEAD_WIDTH), BF16),
        scratch_shapes=[
            pltpu.VMEM((HEAD_WIDTH, 2 * tq), BF16),
            pltpu.VMEM((S // ATTN_TK, VT_ROWS, ATTN_TK), BF16),
            pltpu.VMEM((1, 2 * tq), F32),
            pltpu.VMEM((VT_ROWS, 2 * tq), F32),
            pltpu.VMEM((2, 2, ATTN_TK, tq), F32),
            pltpu.VMEM((2, 1, 2 * tq), F32),
        ],
        compiler_params=pltpu.CompilerParams(
            dimension_semantics=("arbitrary", "arbitrary"),
            vmem_limit_bytes=VMEM_LIMIT),
        name="diff_attn",
    )(q, k, vt, ga, g_subln_col, lq1, lk1, lq2, lk2)


def _post_kernel(x_ref, mp_ref, ya_ref, wout_ref, gc_ref, wcq_ref, kv_ref, wco_ref,
                 gm_ref, wup_ref, wdn_ref, gf_ref, o_ref, oc_ref):
    x1 = (x_ref[0]
          + jnp.dot(mp_ref[0], wout_ref[0:POOL_WIDTH, :], preferred_element_type=F32)
          + jnp.dot(jnp.concatenate([ya_ref[0, hd] for hd in range(DIFF_HEADS)], axis=1),
                    wout_ref[POOL_WIDTH:, :], preferred_element_type=F32))
    hc = _rms(x1, gc_ref[...]).astype(BF16)
    qc = (jnp.dot(hc, wcq_ref[...], preferred_element_type=F32)
          * CROSS_HEAD_DIM ** -0.5).astype(BF16)
    for h in range(CROSS_HEADS):
        sl = slice(h * CROSS_HEAD_DIM, (h + 1) * CROSS_HEAD_DIM)
        kh = kv_ref[0, :, sl]
        vh = kv_ref[0, :, D_MODEL + h * CROSS_HEAD_DIM:D_MODEL + (h + 1) * CROSS_HEAD_DIM]
        s = lax.dot_general(qc[:, sl], kh, (((1,), (1,)), ((), ())),
                            preferred_element_type=F32)
        p = jnp.exp(s - jnp.max(s, axis=1, keepdims=True))
        l = jnp.sum(p, axis=1, keepdims=True)
        oh = jnp.dot(p.astype(BF16), vh, preferred_element_type=F32) / l
        oc_ref[:, sl] = oh.astype(BF16)
    x2 = x1 + jnp.dot(oc_ref[...], wco_ref[...], preferred_element_type=F32)

    hm = _rms(x2, gm_ref[...]).astype(BF16)
    acc = x2
    for c in range(D_FF // FF_CHUNK):
        sl = slice(c * FF_CHUNK, (c + 1) * FF_CHUNK)
        up = jnp.dot(hm, wup_ref[:, sl], preferred_element_type=F32)
        a = jnp.square(jnp.maximum(up, 0.0)).astype(BF16)
        acc = acc + jnp.dot(a, wdn_ref[sl, :], preferred_element_type=F32)
    o_ref[0] = _rms(acc, gf_ref[...])


def _post(x, mp, ya, w_out, g_cross, w_cq, kv, w_co, g_mlp, w_up, w_down, g_final):
    B, S, D = x.shape
    ts = ROW_TILE
    M = kv.shape[1]
    row = lambda b, s: (b, s, 0)
    const2 = lambda b, s: (0, 0)
    resident = lambda shape: pl.BlockSpec(shape, const2, pipeline_mode=pl.Buffered(1))
    return pl.pallas_call(
        _post_kernel,
        grid=(B, S // ts),
        in_specs=[
            pl.BlockSpec((1, ts, D), row),
            pl.BlockSpec((1, ts, POOL_WIDTH), row),
            pl.BlockSpec((1, DIFF_HEADS, ts, HEAD_WIDTH), lambda b, s: (b, 0, s, 0)),
            resident((D, D)),
            resident((1, D)),
            resident((D, D)),
            pl.BlockSpec((1, M, 2 * D), lambda b, s: (b, 0, 0)),
            resident((D, D)),
            resident((1, D)),
            resident((D, D_FF)),
            resident((D_FF, D)),
            resident((1, D)),
        ],
        out_specs=pl.BlockSpec((1, ts, D), row),
        out_shape=jax.ShapeDtypeStruct((B, S, D), F32),
        scratch_shapes=[pltpu.VMEM((ts, D), BF16)],
        compiler_params=pltpu.CompilerParams(
            dimension_semantics=("arbitrary", "arbitrary"),
            vmem_limit_bytes=VMEM_LIMIT),
        name="post",
    )(x, mp, ya, w_out, g_cross, w_cq, kv, w_co, g_mlp, w_up, w_down, g_final)


def _rope_tables(S):
    dh = DIFF_HEAD_DIM
    inv_freq = ROPE_THETA ** (-jnp.arange(0, dh, 2, dtype=F32) / dh)
    inv_freq = jnp.tile(inv_freq, LANES // (dh // 2))
    sign = jnp.where((jnp.arange(LANES) % dh) < dh // 2, -1.0, 1.0).astype(F32)
    assert S % LANES == 0
    ang_hi = (jnp.arange(S // LANES, dtype=F32) * LANES)[:, None] * inv_freq[None, :]
    ang_lo = jnp.arange(LANES, dtype=F32)[:, None] * inv_freq[None, :]
    ch, sh, cl, sl = lax.optimization_barrier(
        (jnp.cos(ang_hi), jnp.sin(ang_hi), jnp.cos(ang_lo), jnp.sin(ang_lo)))
    ch, sh, cl, sl = ch[:, None, :], sh[:, None, :], cl[None, :, :], sl[None, :, :]
    cos_t = (ch * cl - sh * sl).reshape(S, LANES)
    sin_t = ((sh * cl + ch * sl) * sign).reshape(S, LANES)
    return cos_t, sin_t


def kernel(x, mem, g_mix, w_in, w_pool, pool_scale, lambda_q1, lambda_k1, lambda_q2,
           lambda_k2, g_subln, w_out, g_cross, g_mem, w_cq, w_ckv, w_co, g_mlp, w_up,
           w_down, g_final):
    B, S, _ = x.shape
    assert g_mix.shape[0] == 1, "single-layer block"
    cos_t, sin_t = _rope_tables(S)
    mp, q, k, vt, ga = _in_proj(x, g_mix, w_in[0].astype(BF16), w_pool[0].astype(BF16),
                                pool_scale, cos_t, sin_t)
    kv = _mem_kv(mem, g_mem, w_ckv[0].astype(BF16))
    ya = _diff_attn(q, k, vt, ga, g_subln.reshape(HEAD_WIDTH, 1), lambda_q1, lambda_k1,
                    lambda_q2, lambda_k2)
    return _post(x, mp, ya, w_out[0].astype(BF16), g_cross, w_cq[0].astype(BF16), kv,
                 w_co[0].astype(BF16), g_mlp, w_up[0].astype(BF16), w_down[0].astype(BF16),
                 g_final.reshape(1, D_MODEL))
```

```python
import math

import jax
import jax.numpy as jnp
from jax import lax
from jax.experimental import pallas as pl
from jax.experimental.pallas import tpu as pltpu

D_MODEL = 1024
POOL_WIDTH = 512
POOL_WINDOWS = (2, 4, 8, 16)
POOL_GROUP_WIDTH = 128
POOL_HALO = 16
DIFF_HEADS = 4
DIFF_HEAD_DIM = 64
HEAD_WIDTH = 2 * DIFF_HEAD_DIM
Q_WIDTH = DIFF_HEADS * HEAD_WIDTH
Q_LO = POOL_WIDTH
K_LO = Q_LO + Q_WIDTH
V_LO = K_LO + Q_WIDTH
GATE_LO = V_LO + Q_WIDTH
IN_PROJ_WIDTH = GATE_LO + POOL_WIDTH + Q_WIDTH
ROPE_THETA = 10000.0
CROSS_HEADS = 4
CROSS_HEAD_DIM = 256
D_FF = 4096
NORM_EPS = 1e-6
NEG_BIG = -1e30
LAM_INIT = 0.8 - 0.6 * math.exp(-0.3 * 0)
LOG2E = math.log2(math.e)

LANES = 128
BF16_SUBLANES = 16
VMEM_LIMIT = 60 * 1024 * 1024

PROJ_TILE = 1024
ATTN_TQ = 512
ATTN_TK = 512
ATTN_UNROLL = 16
ATTN_TAIL = 4
ROW_TILE = 1024
FF_CHUNK = 1024
VT_ROWS = HEAD_WIDTH + BF16_SUBLANES

BF16 = jnp.bfloat16
F32 = jnp.float32


def _rms(x, g):
    return x * lax.rsqrt(jnp.mean(x * x, axis=-1, keepdims=True) + NORM_EPS) * g


def _rotate_half_pairs(x):
    lane = lax.broadcasted_iota(jnp.int32, x.shape, 1)
    half = DIFF_HEAD_DIM // 2
    lower = (lane % DIFF_HEAD_DIM) < half
    return jnp.where(lower, pltpu.roll(x, LANES - half, axis=1), pltpu.roll(x, half, axis=1))


def _in_proj_kernel(x_ref, g_ref, w_ref, wpool_ref, pscale_ref, cos_ref, sin_ref,
                    mp_ref, q_ref, k_ref, vt_ref, ga_ref, ext_ref):
    si = pl.program_id(1)
    ts = x_ref.shape[1]
    h = _rms(x_ref[0], g_ref[...]).astype(BF16)

    def proj(lo, hi):
        return jnp.dot(h, w_ref[:, lo:hi], preferred_element_type=F32)

    gates = 0.5 + 0.5 * jnp.tanh(0.5 * proj(GATE_LO, IN_PROJ_WIDTH))
    for hd in range(DIFF_HEADS):
        lo = POOL_WIDTH + hd * HEAD_WIDTH
        ga_ref[0, hd] = gates[:, lo:lo + HEAD_WIDTH].astype(BF16)

    @pl.when(si == 0)
    def _():
        ext_ref[0:POOL_HALO, :] = jnp.zeros((POOL_HALO, POOL_WIDTH), F32)

    ext_ref[POOL_HALO:, :] = proj(0, POOL_WIDTH)
    q = proj(Q_LO, K_LO)
    k = proj(K_LO, V_LO)
    v = proj(V_LO, GATE_LO)

    pos = si * ts + lax.broadcasted_iota(jnp.int32, (ts, 1), 0)
    zs = []
    for g, w in enumerate(POOL_WINDOWS):
        lo, hi = g * POOL_GROUP_WIDTH, (g + 1) * POOL_GROUP_WIDTH
        e = ext_ref[:, lo:hi]
        win = e
        shift = 1
        while shift < w:
            win = win + pltpu.roll(win, shift, axis=0)
            shift *= 2
        cnt = jnp.minimum(pos + 1, w).astype(F32)
        zs.append((win[POOL_HALO:] / cnt - e[POOL_HALO:]).astype(BF16))
    ext_ref[0:POOL_HALO, :] = ext_ref[ts:ts + POOL_HALO, :]

    cos = cos_ref[...]
    sin = sin_ref[...]
    qscale = DIFF_HEAD_DIM ** -0.5 * LOG2E
    for c in range(Q_WIDTH // LANES):
        sl = slice(c * LANES, (c + 1) * LANES)
        qc = q[:, sl]
        kc = k[:, sl]
        q_ref[0, c] = ((qc * cos + _rotate_half_pairs(qc) * sin) * qscale).astype(BF16)
        k_ref[0, c] = (kc * cos + _rotate_half_pairs(kc) * sin).astype(BF16)

    for hd in range(DIFF_HEADS):
        vt_ref[0, hd] = v[:, hd * HEAD_WIDTH:(hd + 1) * HEAD_WIDTH].T.astype(BF16)

    for g in range(len(POOL_WINDOWS)):
        lo, hi = g * POOL_GROUP_WIDTH, (g + 1) * POOL_GROUP_WIDTH
        y = jnp.dot(zs[g], wpool_ref[g], preferred_element_type=F32)
        y = y * pscale_ref[:, lo:hi] * gates[:, lo:hi]
        mp_ref[0, :, lo:hi] = y.astype(BF16)


def _in_proj(x, g_mix, w_in, w_pool, pool_scale, cos_t, sin_t):
    B, S, D = x.shape
    ts = PROJ_TILE
    row = lambda b, s: (b, s, 0)
    const2 = lambda b, s: (0, 0)
    half = pl.BlockSpec((1, ts, POOL_WIDTH), row)
    half_sds = jax.ShapeDtypeStruct((B, S, POOL_WIDTH), BF16)
    heads = pl.BlockSpec((1, DIFF_HEADS, ts, HEAD_WIDTH), lambda b, s: (b, 0, s, 0))
    heads_sds = jax.ShapeDtypeStruct((B, DIFF_HEADS, S, HEAD_WIDTH), BF16)
    vt_spec = pl.BlockSpec((1, DIFF_HEADS, HEAD_WIDTH, ts), lambda b, s: (b, 0, 0, s))
    vt_sds = jax.ShapeDtypeStruct((B, DIFF_HEADS, HEAD_WIDTH, S), BF16)
    return pl.pallas_call(
        _in_proj_kernel,
        grid=(B, S // ts),
        in_specs=[
            pl.BlockSpec((1, ts, D), row),
            pl.BlockSpec((1, D), const2),
            pl.BlockSpec((D, IN_PROJ_WIDTH), const2),
            pl.BlockSpec((len(POOL_WINDOWS), POOL_GROUP_WIDTH, POOL_GROUP_WIDTH),
                         lambda b, s: (0, 0, 0)),
            pl.BlockSpec((1, POOL_WIDTH), const2),
            pl.BlockSpec((ts, LANES), lambda b, s: (s, 0)),
            pl.BlockSpec((ts, LANES), lambda b, s: (s, 0)),
        ],
        out_specs=[half, heads, heads, vt_spec, heads],
        out_shape=[half_sds, heads_sds, heads_sds, vt_sds, heads_sds],
        scratch_shapes=[pltpu.VMEM((ts + POOL_HALO, POOL_WIDTH), F32)],
        compiler_params=pltpu.CompilerParams(
            dimension_semantics=("arbitrary", "arbitrary"),
            vmem_limit_bytes=VMEM_LIMIT),
        name="in_proj",
    )(x, g_mix, w_in, w_pool, pool_scale, cos_t, sin_t)


def _mem_kv_kernel(mem_ref, g_ref, w_ref, kv_ref):
    mn = _rms(mem_ref[0], g_ref[...]).astype(BF16)
    kv_ref[0] = jnp.dot(mn, w_ref[...], preferred_element_type=F32).astype(BF16)


def _mem_kv(mem, g_mem, w_ckv):
    B, M, D = mem.shape
    return pl.pallas_call(
        _mem_kv_kernel,
        grid=(B,),
        in_specs=[
            pl.BlockSpec((1, M, D), lambda b: (b, 0, 0)),
            pl.BlockSpec((1, D), lambda b: (0, 0)),
            pl.BlockSpec((D, 2 * D), lambda b: (0, 0)),
        ],
        out_specs=pl.BlockSpec((1, M, 2 * D), lambda b: (b, 0, 0)),
        out_shape=jax.ShapeDtypeStruct((B, M, 2 * D), BF16),
        compiler_params=pltpu.CompilerParams(
            dimension_semantics=("arbitrary",), vmem_limit_bytes=VMEM_LIMIT),
        name="mem_kv",
    )(mem, g_mem, w_ckv)


def _diff_attn_kernel(q_ref, k_ref, vt_ref, ga_ref, gsub_ref, lq1_ref, lk1_ref,
                      lq2_ref, lk2_ref, o_ref, qs_ref, vx_ref, m_ref, acc_ref, st_ref, mc_ref):
    tq = ATTN_TQ
    tk = ATTN_TK
    nkb = vx_ref.shape[0]
    nq = q_ref.shape[2] // tq

    for j in range(nkb):
        vx_ref[j, 0:HEAD_WIDTH, :] = vt_ref[0, 0, :, j * tk:(j + 1) * tk]
        vx_ref[j, HEAD_WIDTH:, :] = jnp.ones((BF16_SUBLANES, tk), BF16)

    lam = (jnp.exp(jnp.sum(lq1_ref[...] * lk1_ref[...], keepdims=True))
           - jnp.exp(jnp.sum(lq2_ref[...] * lk2_ref[...], keepdims=True)) + LAM_INIT)

    def scores(kb, c, slot):
        start = pl.multiple_of(kb * tk, tk)
        st = jnp.dot(k_ref[0, 0, pl.ds(start, tk), :], qs_ref[:, c * tq:(c + 1) * tq],
                     preferred_element_type=F32)
        st_ref[slot, c] = st
        mc_ref[slot, :, c * tq:(c + 1) * tq] = jnp.max(st, axis=0, keepdims=True)

    def softmax(slot, c, masked):
        cols = slice(c * tq, (c + 1) * tq)
        st = st_ref[slot, c]
        if masked:
            key = lax.broadcasted_iota(jnp.int32, st.shape, 0)
            qry = lax.broadcasted_iota(jnp.int32, st.shape, 1)
            st = jnp.where(key <= qry, st, NEG_BIG)
            m_cur = jnp.max(st, axis=0, keepdims=True)
        else:
            m_cur = mc_ref[slot, :, cols]
        m_prev = m_ref[:, cols]
        m_new = jnp.maximum(m_prev, m_cur)
        m_ref[:, cols] = m_new
        return jnp.exp2(m_prev - m_new), jnp.exp2(st - m_new).astype(BF16)

    def pv(kb, c, alpha, pt):
        cols = slice(c * tq, (c + 1) * tq)
        acc_ref[:, cols] = alpha * acc_ref[:, cols] + jnp.dot(
            vx_ref[kb], pt, preferred_element_type=F32)

    def block(kb, slot, masked):
        if not masked:
            scores(kb + 1, 0, 1 - slot)
        a0, p0 = softmax(slot, 0, masked)
        pv(kb, 0, a0, p0)
        if not masked:
            scores(kb + 1, 1, 1 - slot)
        a1, p1 = softmax(slot, 1, masked)
        pv(kb, 1, a1, p1)

    def group(base, n):
        for j in range(n):
            block(base + j, j % 2, False)

    def rows_of(qi):
        return pl.ds(pl.multiple_of(qi * tq, tq), tq)

    def start_tile(qi):
        qt = q_ref[0, 0, rows_of(qi), :].astype(F32).T
        first = lax.broadcasted_iota(jnp.int32, qt.shape, 0) < DIFF_HEAD_DIM
        qs_ref[:, 0:tq] = jnp.where(first, qt, 0.0).astype(BF16)
        qs_ref[:, tq:] = jnp.where(first, 0.0, qt).astype(BF16)
        m_ref[...] = jnp.full(m_ref.shape, NEG_BIG, F32)
        acc_ref[...] = jnp.zeros(acc_ref.shape, F32)
        scores(0, 0, 0)
        scores(0, 1, 0)

    def key_blocks(qi):
        def body(u, carry):
            group(ATTN_UNROLL * u, ATTN_UNROLL)
            return carry

        lax.fori_loop(0, qi // ATTN_UNROLL, body, 0)
        done = (qi // ATTN_UNROLL) * ATTN_UNROLL
        size = ATTN_UNROLL // 2
        while size >= ATTN_TAIL:
            take = (qi - done) >= size

            @pl.when(take)
            def _(done=done, size=size):
                group(done, size)

            done = done + jnp.where(take, size, 0)
            size //= 2
        for r in range(ATTN_TAIL):
            @pl.when(qi - done == r)
            def _(r=r, done=done):
                group(done, r)
                block(qi, r % 2, True)
                finish_tile(qi)
                start_tile(jnp.minimum(qi + 1, nq - 1))

    def finish_tile(qi):
        o1 = acc_ref[0:HEAD_WIDTH, 0:tq] / acc_ref[HEAD_WIDTH:HEAD_WIDTH + 1, 0:tq]
        o2 = acc_ref[0:HEAD_WIDTH, tq:] / acc_ref[HEAD_WIDTH:HEAD_WIDTH + 1, tq:]
        ot = o1 - lam * o2
        inv = lax.rsqrt(jnp.mean(ot * ot, axis=0, keepdims=True) + NORM_EPS)
        yt = ot * inv * gsub_ref[...] * (1.0 - LAM_INIT)
        rows = rows_of(qi)
        o_ref[0, 0, rows, :] = (yt.T * ga_ref[0, 0, rows, :].astype(F32)).astype(BF16)

    start_tile(0)

    def tile(qi, carry):
        key_blocks(qi)
        return carry

    lax.fori_loop(0, nq, tile, 0)


def _diff_attn(q, k, vt, ga, g_subln_col, lq1, lk1, lq2, lk2):
    B, _, S, _ = q.shape
    tq = ATTN_TQ
    assert tq == ATTN_TK and S % tq == 0
    seq = pl.BlockSpec((1, 1, S, HEAD_WIDTH), lambda b, h: (b, h, 0, 0))
    vtblk = pl.BlockSpec((1, 1, HEAD_WIDTH, S), lambda b, h: (b, h, 0, 0))
    vec = lambda n: pl.BlockSpec((1, n), lambda b, h: (0, 0))
    return pl.pallas_call(
        _diff_attn_kernel,
        grid=(B, DIFF_HEADS),
        in_specs=[seq, seq, vtblk, seq,
                  pl.BlockSpec((HEAD_WIDTH, 1), lambda b, h: (0, 0)),
                  vec(DIFF_HEAD_DIM), vec(DIFF_HEAD_DIM), vec(DIFF_HEAD_DIM), vec(DIFF_HEAD_DIM)],
        out_specs=seq,
        out_shape=jax.ShapeDtypeStruct((B, DIFF_HEADS, S, HEAD_WIDTH), BF16),
        scratch_shapes=[
            pltpu.VMEM((HEAD_WIDTH, 2 * tq), BF16),
            pltpu.VMEM((S // ATTN_TK, VT_ROWS, ATTN_TK), BF16),
            pltpu.VMEM((1, 2 * tq), F32),
            pltpu.VMEM((VT_ROWS, 2 * tq), F32),
            pltpu.VMEM((2, 2, ATTN_TK, tq), F32),
            pltpu.VMEM((2, 1, 2 * tq), F32),
        ],
        compiler_params=pltpu.CompilerParams(
            dimension_semantics=("arbitrary", "arbitrary"),
            vmem_limit_bytes=VMEM_LIMIT),
        name="diff_attn",
    )(q, k, vt, ga, g_subln_col, lq1, lk1, lq2, lk2)


def _post_kernel(x_ref, mp_ref, ya_ref, wout_ref, gc_ref, wcq_ref, kv_ref, wco_ref,
                 gm_ref, wup_ref, wdn_ref, gf_ref, o_ref, oc_ref):
    x1 = (x_ref[0]
          + jnp.dot(mp_ref[0], wout_ref[0:POOL_WIDTH, :], preferred_element_type=F32)
          + jnp.dot(jnp.concatenate([ya_ref[0, hd] for hd in range(DIFF_HEADS)], axis=1),
                    wout_ref[POOL_WIDTH:, :], preferred_element_type=F32))
    hc = _rms(x1, gc_ref[...]).astype(BF16)
    qc = (jnp.dot(hc, wcq_ref[...], preferred_element_type=F32)
          * CROSS_HEAD_DIM ** -0.5).astype(BF16)
    for h in range(CROSS_HEADS):
        sl = slice(h * CROSS_HEAD_DIM, (h + 1) * CROSS_HEAD_DIM)
        kh = kv_ref[0, :, sl]
        vh = kv_ref[0, :, D_MODEL + h * CROSS_HEAD_DIM:D_MODEL + (h + 1) * CROSS_HEAD_DIM]
        s = lax.dot_general(qc[:, sl], kh, (((1,), (1,)), ((), ())),
                            preferred_element_type=F32)
        p = jnp.exp(s - jnp.max(s, axis=1, keepdims=True))
        l = jnp.sum(p, axis=1, keepdims=True)
        oh = jnp.dot(p.astype(BF16), vh, preferred_element_type=F32) / l
        oc_ref[:, sl] = oh.astype(BF16)
    x2 = x1 + jnp.dot(oc_ref[...], wco_ref[...], preferred_element_type=F32)

    hm = _rms(x2, gm_ref[...]).astype(BF16)
    acc = x2
    for c in range(D_FF // FF_CHUNK):
        sl = slice(c * FF_CHUNK, (c + 1) * FF_CHUNK)
        up = jnp.dot(hm, wup_ref[:, sl], preferred_element_type=F32)
        a = jnp.square(jnp.maximum(up, 0.0)).astype(BF16)
        acc = acc + jnp.dot(a, wdn_ref[sl, :], preferred_element_type=F32)
    o_ref[0] = _rms(acc, gf_ref[...])


def _post(x, mp, ya, w_out, g_cross, w_cq, kv, w_co, g_mlp, w_up, w_down, g_final):
    B, S, D = x.shape
    ts = ROW_TILE
    M = kv.shape[1]
    row = lambda b, s: (b, s, 0)
    const2 = lambda b, s: (0, 0)
    resident = lambda shape: pl.BlockSpec(shape, const2, pipeline_mode=pl.Buffered(1))
    return pl.pallas_call(
        _post_kernel,
        grid=(B, S // ts),
        in_specs=[
            pl.BlockSpec((1, ts, D), row),
            pl.BlockSpec((1, ts, POOL_WIDTH), row),
            pl.BlockSpec((1, DIFF_HEADS, ts, HEAD_WIDTH), lambda b, s: (b, 0, s, 0)),
            resident((D, D)),
            resident((1, D)),
            resident((D, D)),
            pl.BlockSpec((1, M, 2 * D), lambda b, s: (b, 0, 0)),
            resident((D, D)),
            resident((1, D)),
            resident((D, D_FF)),
            resident((D_FF, D)),
            resident((1, D)),
        ],
        out_specs=pl.BlockSpec((1, ts, D), row),
        out_shape=jax.ShapeDtypeStruct((B, S, D), F32),
        scratch_shapes=[pltpu.VMEM((ts, D), BF16)],
        compiler_params=pltpu.CompilerParams(
            dimension_semantics=("arbitrary", "arbitrary"),
            vmem_limit_bytes=VMEM_LIMIT),
        name="post",
    )(x, mp, ya, w_out, g_cross, w_cq, kv, w_co, g_mlp, w_up, w_down, g_final)


def _rope_tables(S):
    dh = DIFF_HEAD_DIM
    inv_freq = ROPE_THETA ** (-jnp.arange(0, dh, 2, dtype=F32) / dh)
    inv_freq = jnp.tile(inv_freq, LANES // (dh // 2))
    sign = jnp.where((jnp.arange(LANES) % dh) < dh // 2, -1.0, 1.0).astype(F32)
    assert S % LANES == 0
    ang_hi = (jnp.arange(S // LANES, dtype=F32) * LANES)[:, None] * inv_freq[None, :]
    ang_lo = jnp.arange(LANES, dtype=F32)[:, None] * inv_freq[None, :]
    ch, sh, cl, sl = lax.optimization_barrier(
        (jnp.cos(ang_hi), jnp.sin(ang_hi), jnp.cos(ang_lo), jnp.sin(ang_lo)))
    ch, sh, cl, sl = ch[:, None, :], sh[:, None, :], cl[None, :, :], sl[None, :, :]
    cos_t = (ch * cl - sh * sl).reshape(S, LANES)
    sin_t = ((sh * cl + ch * sl) * sign).reshape(S, LANES)
    return cos_t, sin_t


def kernel(x, mem, g_mix, w_in, w_pool, pool_scale, lambda_q1, lambda_k1, lambda_q2,
           lambda_k2, g_subln, w_out, g_cross, g_mem, w_cq, w_ckv, w_co, g_mlp, w_up,
           w_down, g_final):
    B, S, _ = x.shape
    assert g_mix.shape[0] == 1, "single-layer block"
    cos_t, sin_t = _rope_tables(S)
    mp, q, k, vt, ga = _in_proj(x, g_mix, w_in[0].astype(BF16), w_pool[0].astype(BF16),
                                pool_scale, cos_t, sin_t)
    kv = _mem_kv(mem, g_mem, w_ckv[0].astype(BF16))
    ya = _diff_attn(q, k, vt, ga, g_subln.reshape(HEAD_WIDTH, 1), lambda_q1, lambda_k1,
                    lambda_q2, lambda_k2)
    return _post(x, mp, ya, w_out[0].astype(BF16), g_cross, w_cq[0].astype(BF16), kv,
                 w_co[0].astype(BF16), g_mlp, w_up[0].astype(BF16), w_down[0].astype(BF16),
                 g_final.reshape(1, D_MODEL))
```
